```python
import math
import jax
import jax.numpy as jnp
from jax import lax
import numpy as np

D_MODEL = 1024
BATCH = 4
SEQ = 8192
DEPTH = 2

HEAD_DIM = 64
ATT_HEADS = (3 * D_MODEL) // (8 * HEAD_DIM)
ATT_KV_HEADS = ATT_HEADS // 3
ATT_WIDTH = ATT_HEADS * HEAD_DIM
ATT_KV_WIDTH = ATT_KV_HEADS * HEAD_DIM
WINDOW = 128
BLOCK = 128
ROPE_DIM = HEAD_DIM // 4
ROPE_THETA = 500000.0
DN_HEADS = (3 * D_MODEL) // (8 * HEAD_DIM)
DN_WIDTH = DN_HEADS * HEAD_DIM
DN_CONV = 5
DN_CHUNK = 64
RW_HEADS = (D_MODEL - ATT_WIDTH - DN_WIDTH) // HEAD_DIM
RW_WIDTH = RW_HEADS * HEAD_DIM
W_RANK = 64
A_RANK = 64
G_RANK = 128
RW_IN = 3 * RW_WIDTH + W_RANK + A_RANK + G_RANK
D_FF = (11 * D_MODEL) // 4
FFN_CONV = 3
NORM_EPS = 1e-6
RW_LN_EPS = 64e-5
IN_SPLITS = (ATT_WIDTH, ATT_KV_WIDTH, ATT_KV_WIDTH, 3 * DN_WIDTH, DN_WIDTH,
             2 * DN_HEADS, 2 * DN_HEADS, RW_IN)
N_IN = sum(IN_SPLITS)

kernel_name = 'hybrid_parallel_head_encoder'


def _split(t, sizes):
    idx = [int(i) for i in np.cumsum(sizes)[:-1]]
    return jnp.split(t, idx, axis=-1)


def rms_norm(x, g):
    xf = x.astype(jnp.float32)
    y = xf * lax.rsqrt(jnp.mean(xf * xf, -1, keepdims=True) + NORM_EPS)
    return (y * g.astype(jnp.float32)).astype(x.dtype)


def l2norm(t):
    return t * lax.rsqrt(jnp.sum(t * t, -1, keepdims=True) + 1e-6)


def depthwise_conv(x, w):
    pad = (w.shape[0] - 1) // 2
    return lax.conv_general_dilated(
        x, w[:, None, :].astype(x.dtype), window_strides=(1,), padding=[(pad, pad)],
        dimension_numbers=('NWC', 'WIO', 'NWC'), feature_group_count=x.shape[-1])


def partial_rotary(t, positions):
    half = ROPE_DIM // 2
    inv_freq = ROPE_THETA ** (-jnp.arange(half, dtype=jnp.float32) / half)
    ang = positions.astype(jnp.float32)[:, :, None, None] * inv_freq
    cos, sin = jnp.cos(ang), jnp.sin(ang)
    tf = t.astype(jnp.float32)
    t1, t2, rest = tf[..., :half], tf[..., half:ROPE_DIM], tf[..., ROPE_DIM:]
    out = jnp.concatenate([t1 * cos - t2 * sin, t2 * cos + t1 * sin, rest], -1)
    return out.astype(t.dtype)


def banded_window_attention(q, k, v, sink):
    B, S, _, _ = q.shape
    nb = S // BLOCK
    G = ATT_HEADS // ATT_KV_HEADS
    qb = q.reshape(B, nb, BLOCK, ATT_KV_HEADS, G, HEAD_DIM)

    def windows(t):
        tb = t.reshape(B, nb, BLOCK, ATT_KV_HEADS, HEAD_DIM)
        tp = jnp.pad(tb, ((0, 0), (1, 1), (0, 0), (0, 0), (0, 0)))
        return jnp.concatenate([tp[:, :-2], tp[:, 1:-1], tp[:, 2:]], axis=2)

    kw, vw = windows(k), windows(v)
    s = jnp.einsum('bnqhgd,bnkhd->bnhgqk', qb, kw,
                   preferred_element_type=jnp.float32) * (HEAD_DIM ** -0.5)
    qpos = jnp.arange(BLOCK)[:, None]
    kpos = jnp.arange(3 * BLOCK)[None, :] - BLOCK
    abs_k = jnp.arange(nb)[:, None, None] * BLOCK + kpos
    valid = (jnp.abs(kpos - qpos) <= WINDOW)[None] & (abs_k >= 0) & (abs_k < S)
    s = jnp.where(valid[None, :, None, None], s, -jnp.inf)
    sk = sink.astype(jnp.float32).reshape(1, 1, ATT_KV_HEADS, G, 1, 1)
    m = jnp.maximum(jnp.max(s, -1, keepdims=True), sk)
    p = jnp.exp(s - m)
    p = p / (jnp.sum(p, -1, keepdims=True) + jnp.exp(sk - m))
    o = jnp.einsum('bnhgqk,bnkhd->bnqhgd', p.astype(v.dtype), vw)
    return o.reshape(B, S, ATT_WIDTH)


def attention_mixer(q, k, v, positions, sink):
    B, S, _ = q.shape
    q = partial_rotary(q.reshape(B, S, ATT_HEADS, HEAD_DIM), positions)
    k = partial_rotary(k.reshape(B, S, ATT_KV_HEADS, HEAD_DIM), positions)
    v = v.reshape(B, S, ATT_KV_HEADS, HEAD_DIM)
    return banded_window_attention(q, k, v, sink)


def gated_delta_chunked(q, k, v, g, beta):
    B, H, S, Dk = q.shape
    Dv = v.shape[-1]
    C = DN_CHUNK
    N = S // C
    q = q * (Dk ** -0.5)
    q, k, v = [t.reshape(B, H, N, C, t.shape[-1]) for t in (q, k, v)]
    g, beta = g.reshape(B, H, N, C), beta.reshape(B, H, N, C)
    gc = jnp.cumsum(g, axis=-1)
    tril = jnp.tril(jnp.ones((C, C), bool))
    strict = jnp.tril(jnp.ones((C, C), bool), -1)
    decay = jnp.exp(jnp.where(tril, gc[..., :, None] - gc[..., None, :], -jnp.inf))
    k_beta = k * beta[..., None]
    lower = jnp.where(strict, jnp.einsum('bhnid,bhnjd->bhnij', k_beta, k) * decay, 0.0)
    a_mat = lower + jnp.eye(C, dtype=lower.dtype)
    rhs = jnp.concatenate([v * beta[..., None], k_beta * jnp.exp(gc)[..., None]], -1)
    sol = lax.linalg.triangular_solve(a_mat, rhs, left_side=True, lower=True, unit_diagonal=True)
    value, k_cum = sol[..., :Dv], sol[..., Dv:]
    qk = jnp.einsum('bhnid,bhnjd->bhnij', q, k) * decay
    q_g = q * jnp.exp(gc)[..., None]
    k_g = k * jnp.exp(gc[..., -1:] - gc)[..., None]
    g_last = jnp.exp(gc[..., -1])

    def step(state, inp):
        value_i, kcum_i, qk_i, qg_i, kg_i, gl_i = inp
        v_new = value_i - jnp.einsum('bhcd,bhde->bhce', kcum_i, state)
        o = jnp.einsum('bhcd,bhde->bhce', qg_i, state) + jnp.einsum('bhij,bhje->bhie', qk_i, v_new)
        state = state * gl_i[..., None, None] + jnp.einsum('bhcd,bhce->bhde', kg_i, v_new)
        return state, o

    xs = tuple(jnp.moveaxis(t, 2, 0) for t in (value, k_cum, qk, q_g, k_g, g_last))
    _, o = lax.scan(step, jnp.zeros((B, H, Dk, Dv), jnp.float32), xs)
    return jnp.moveaxis(o, 0, 2).reshape(B, H, S, Dv)


def deltanet_mixer(qkv, z, alpha, beta_logit, conv_w, a_log, dt_bias, norm_w):
    B, S, _ = qkv.shape
    f32 = jnp.float32
    qkv = jax.nn.silu(depthwise_conv(qkv, conv_w)).astype(f32)
    q, k, v = [t.reshape(B, S, DN_HEADS, HEAD_DIM).transpose(0, 2, 1, 3)
               for t in jnp.split(qkv, 3, axis=-1)]
    q, k = l2norm(q), l2norm(k)
    alpha = alpha.astype(f32).reshape(B, S, 2, DN_HEADS)
    g = -jnp.exp(a_log.astype(f32)) * jax.nn.softplus(alpha + dt_bias.astype(f32))
    beta = jax.nn.sigmoid(beta_logit.astype(f32).reshape(B, S, 2, DN_HEADS))
    g, beta = g.transpose(2, 0, 3, 1), beta.transpose(2, 0, 3, 1)
    o_fwd = gated_delta_chunked(q, k, v, g[0], beta[0])
    rev = lambda t: jnp.flip(t, axis=2)
    o_bwd = rev(gated_delta_chunked(rev(q), rev(k), rev(v), rev(g[1]), rev(beta[1])))
    o = (o_fwd + o_bwd).transpose(0, 2, 1, 3)
    o = o * lax.rsqrt(jnp.mean(o * o, -1, keepdims=True) + NORM_EPS) * norm_w.astype(f32)
    o = o * jax.nn.silu(z.astype(f32).reshape(B, S, DN_HEADS, HEAD_DIM))
    return o.reshape(B, S, DN_WIDTH).astype(z.dtype)


def rwkv7_scan(r, w, k, v, a, b):
    B, S, H, N = r.shape
    xs = tuple(jnp.moveaxis(t, 1, 0) for t in (r, w, k, v, a, b))

    def step(state, inp):
        r_t, w_t, k_t, v_t, a_t, b_t = inp
        sa = jnp.einsum('bhvk,bhk->bhv', state, a_t)
        state = (state * w_t[:, :, None, :] + sa[..., None] * b_t[:, :, None, :]
                 + v_t[..., None] * k_t[:, :, None, :])
        return state, jnp.einsum('bhvk,bhk->bhv', state, r_t)

    _, y = lax.scan(step, jnp.zeros((B, H, N, N), jnp.float32), xs)
    return jnp.moveaxis(y, 0, 1)


def rwkv_mixer(p, mu, w0, w_up, a0, a_up, g_up, k_k, k_a, r_k, ln_w, ln_b):
    B, S, _ = p.shape
    out_dtype = p.dtype
    f32 = jnp.float32
    p = p.astype(f32)
    prev = jnp.pad(p, ((0, 0), (1, 0), (0, 0)))[:, :-1]
    nxt = jnp.pad(p, ((0, 0), (0, 1), (0, 0)))[:, 1:]
    p = p + mu[0].astype(f32) * (prev - p) + mu[1].astype(f32) * (nxt - p)
    r, k, v, wd, ad, gd = _split(p, (RW_WIDTH, RW_WIDTH, RW_WIDTH, W_RANK, A_RANK, G_RANK))
    wl = jnp.tanh(wd)
    g = jax.nn.sigmoid(gd) @ g_up.astype(f32)
    heads = lambda t: t.reshape(B, S, RW_HEADS, HEAD_DIM)
    kk = l2norm(heads(k * k_k.astype(f32)))

    def direction(d):
        w = -jax.nn.softplus(-(w0[d].astype(f32) + wl @ w_up[d].astype(f32))) - 0.5
        decay = jnp.exp(-jnp.exp(w))
        a = jax.nn.sigmoid(a0[d].astype(f32) + ad @ a_up[d].astype(f32))
        kd = k * (1.0 + (a - 1.0) * k_a.astype(f32))
        return heads(decay), heads(kd), heads(a)

    rh, vh = heads(r), heads(v)
    dec_f, k_f, a_f = direction(0)
    y_f = rwkv7_scan(rh, dec_f, k_f, vh, -kk, kk * a_f)
    dec_b, k_b, a_b = direction(1)
    rev = lambda t: jnp.flip(t, axis=1)
    y_b = rev(rwkv7_scan(rev(rh), rev(dec_b), rev(k_b), rev(vh), rev(-kk), rev(kk * a_b)))
    y = y_f + y_b
    mean = jnp.mean(y, -1, keepdims=True)
    var = jnp.mean(jnp.square(y - mean), -1, keepdims=True)
    y = ((y - mean) * lax.rsqrt(var + RW_LN_EPS)).reshape(B, S, RW_WIDTH) * ln_w.astype(f32) + ln_b.astype(f32)
    bonus = jnp.sum(rh * heads(k) * r_k.astype(f32), -1, keepdims=True) * vh
    return ((y + bonus.reshape(B, S, RW_WIDTH)) * g).astype(out_dtype)


def conv_ffn(h, w_up, conv_w, w_down):
    u = depthwise_conv(h @ w_up, conv_w)
    gate, val = jnp.split(u, 2, axis=-1)
    return (jax.nn.silu(gate) * val) @ w_down


def setup_inputs(seed: int = 0) -> dict:
    key = jax.random.key(seed)
    ks = jax.random.split(key, 32)
    f32 = jnp.float32
    L = DEPTH

    def nrm(i, shape, scale):
        return scale * jax.random.normal(ks[i], shape, f32)

    def uni(i, shape, lo, hi):
        return jax.random.uniform(ks[i], shape, f32, lo, hi)

    x = nrm(0, (BATCH, SEQ, D_MODEL), 1.0)
    positions = (jnp.arange(SEQ, dtype=jnp.int32)[None, :]
                 + jax.random.randint(ks[1], (BATCH, 1), 0, SEQ, dtype=jnp.int32))
    norm_mix = 1.0 + nrm(2, (L, D_MODEL), 0.02)
    w_in = nrm(3, (L, D_MODEL, N_IN), D_MODEL ** -0.5)
    attn_sink = nrm(4, (L, ATT_HEADS), 0.5)
    dn_conv = nrm(5, (L, DN_CONV, 3 * DN_WIDTH), DN_CONV ** -0.5)
    dn_a_log = jnp.log(uni(6, (L, 2, DN_HEADS), 1.0, 16.0))
    dt = jnp.exp(uni(7, (L, 2, DN_HEADS), math.log(1e-3), math.log(1e-1)))
    dn_dt_bias = dt + jnp.log(-jnp.expm1(-dt))
    dn_norm = 1.0 + nrm(8, (L, HEAD_DIM), 0.02)
    rw_mu = uni(9, (L, 2, RW_IN), 0.0, 0.5)
    rw_w0 = uni(10, (L, 2, RW_WIDTH), -2.0, 1.0)
    rw_w_up = nrm(11, (L, 2, W_RANK, RW_WIDTH), 0.5 * W_RANK ** -0.5)
    rw_a0 = nrm(12, (L, 2, RW_WIDTH), 0.1)
    rw_a_up = nrm(13, (L, 2, A_RANK, RW_WIDTH), 0.5 * A_RANK ** -0.5)
    rw_g_up = nrm(14, (L, G_RANK, RW_WIDTH), G_RANK ** -0.5)
    rw_k_k = 0.85 + nrm(15, (L, RW_WIDTH), 0.02)
    rw_k_a = 1.0 + nrm(16, (L, RW_WIDTH), 0.02)
    rw_r_k = nrm(17, (L, RW_HEADS, HEAD_DIM), 0.1)
    rw_ln_w = 1.0 + nrm(18, (L, RW_WIDTH), 0.02)
    rw_ln_b = nrm(19, (L, RW_WIDTH), 0.01)
    w_out = nrm(20, (L, D_MODEL, D_MODEL), D_MODEL ** -0.5)
    norm_ffn = 1.0 + nrm(21, (L, D_MODEL), 0.02)
    ffn_w_up = nrm(22, (L, D_MODEL, 2 * D_FF), D_MODEL ** -0.5)
    ffn_conv = nrm(23, (L, FFN_CONV, 2 * D_FF), FFN_CONV ** -0.5)
    ffn_w_down = nrm(24, (L, D_FF, D_MODEL), D_FF ** -0.5)
    norm_final = 1.0 + nrm(25, (D_MODEL,), 0.02)
    return {'x': x, 'positions': positions, 'norm_mix': norm_mix, 'w_in': w_in,
            'attn_sink': attn_sink, 'dn_conv': dn_conv, 'dn_a_log': dn_a_log,
            'dn_dt_bias': dn_dt_bias, 'dn_norm': dn_norm, 'rw_mu': rw_mu, 'rw_w0': rw_w0,
            'rw_w_up': rw_w_up, 'rw_a0': rw_a0, 'rw_a_up': rw_a_up, 'rw_g_up': rw_g_up,
            'rw_k_k': rw_k_k, 'rw_k_a': rw_k_a, 'rw_r_k': rw_r_k, 'rw_ln_w': rw_ln_w,
            'rw_ln_b': rw_ln_b, 'w_out': w_out, 'norm_ffn': norm_ffn, 'ffn_w_up': ffn_w_up,
            'ffn_conv': ffn_conv, 'ffn_w_down': ffn_w_down, 'norm_final': norm_final}


def reference(x, positions, norm_mix, w_in, attn_sink, dn_conv, dn_a_log, dn_dt_bias, dn_norm,
              rw_mu, rw_w0, rw_w_up, rw_a0, rw_a_up, rw_g_up, rw_k_k, rw_k_a, rw_r_k,
              rw_ln_w, rw_ln_b, w_out, norm_ffn, ffn_w_up, ffn_conv, ffn_w_down, norm_final):
    for l in range(DEPTH):
        h = rms_norm(x, norm_mix[l])
        proj = h @ w_in[l]
        aq, ak, av, dqkv, dz, dalpha, dbeta, rwp = _split(proj, IN_SPLITS)
        y_att = attention_mixer(aq, ak, av, positions, attn_sink[l])
        y_dn = deltanet_mixer(dqkv, dz, dalpha, dbeta, dn_conv[l], dn_a_log[l],
                              dn_dt_bias[l], dn_norm[l])
        y_rw = rwkv_mixer(rwp, rw_mu[l], rw_w0[l], rw_w_up[l], rw_a0[l], rw_a_up[l],
                          rw_g_up[l], rw_k_k[l], rw_k_a[l], rw_r_k[l],
                          rw_ln_w[l], rw_ln_b[l])
        mixed = jnp.concatenate([y_att.astype(x.dtype), y_dn.astype(x.dtype),
                                 y_rw.astype(x.dtype)], axis=-1)
        x = x + mixed @ w_out[l]
        h = rms_norm(x, norm_ffn[l])
        x = x + conv_ffn(h, ffn_w_up[l], ffn_conv[l], ffn_w_down[l])
    return rms_norm(x, norm_final)
```

```python
import functools

import jax
import jax.numpy as jnp
from jax import lax
from jax.experimental import pallas as pl
from jax.experimental.pallas import tpu as pltpu

F32 = jnp.float32
BF16 = jnp.bfloat16

D_MODEL = 1024
HEAD_DIM = 64
HEAD_SHIFT = 6
ATT_HEADS = 6
ATT_WIDTH = 384
ATT_KV_WIDTH = 128
WINDOW = 128
ROPE_DIM = 16
ROPE_THETA = 500000.0
DN_HEADS = 6
DN_WIDTH = 384
DN_CONV = 5
RW_HEADS = 4
RW_WIDTH = 256
W_RANK = 64
A_RANK = 64
G_RANK = 128
RW_IN = 1024
D_FF = 2816
NORM_EPS = 1e-6
RW_LN_EPS = 64e-5
LANES = 128
CHUNK = 64
NEG_BIG = -1e30

NN = (((1,), (0,)), ((), ()))
NT = (((1,), (1,)), ((), ()))
TN = (((0,), (0,)), ((), ()))

VMEM_LIMIT = 56 * 1024 * 1024


def _mm(a, b, dims=NN):
    return lax.dot_general(a, b, dims, preferred_element_type=F32)


def _split2(x):
    hi = x.astype(BF16)
    lo = (x - hi.astype(F32)).astype(BF16)
    return hi, lo


def _split3(x):
    hi = x.astype(BF16)
    r = x - hi.astype(F32)
    mid = r.astype(BF16)
    lo = (r - mid.astype(F32)).astype(BF16)
    return hi, mid, lo


def _mm3(a, b, dims=NN):
    ah, al = _split2(a)
    bh, bl = _split2(b)
    return _mm(ah, bh, dims) + (_mm(ah, bl, dims) + _mm(al, bh, dims))


def _mmx(a, e, dims=NN):
    h, m, l = _split3(a)
    return _mm(h, e, dims) + (_mm(m, e, dims) + _mm(l, e, dims))


def _xmm(e, a, dims=NN):
    h, m, l = _split3(a)
    return _mm(e, h, dims) + (_mm(e, m, dims) + _mm(e, l, dims))


def _iota(shape, dim):
    return lax.broadcasted_iota(jnp.int32, shape, dim)


def _sigmoid(x):
    return 1.0 / (1.0 + jnp.exp(-x))


def _softplus(x):
    return jnp.maximum(x, 0.0) + jnp.log1p(jnp.exp(-jnp.abs(x)))


def _head_ones(width):
    r = _iota((width, width), 0) >> HEAD_SHIFT
    c = _iota((width, width), 1) >> HEAD_SHIFT
    return jnp.where(r == c, 1.0, 0.0).astype(BF16)


def _params(sem, vmem=VMEM_LIMIT):
    return pltpu.CompilerParams(dimension_semantics=sem, vmem_limit_bytes=vmem)


def _rope_table_kernel(pos_ref, freq_ref, c_ref, s1_ref, s2_ref):
    ang = pos_ref[...].astype(F32) * freq_ref[...]
    c = jnp.cos(ang)
    s = jnp.sin(ang)
    j = _iota(ang.shape, 1) & (HEAD_DIM - 1)
    half = ROPE_DIM // 2
    c_ref[...] = c
    s1_ref[...] = jnp.where(j < half, -s, 0.0)
    s2_ref[...] = jnp.where((j >= half) & (j < ROPE_DIM), s, 0.0)


def _rope_tables(positions):
    T = positions.size
    half = ROPE_DIM // 2
    inv_freq = ROPE_THETA ** (-jnp.arange(half, dtype=F32) / half)
    lane = jnp.arange(LANES)
    freq = jnp.where((lane % HEAD_DIM) < ROPE_DIM, inv_freq[lane % half], 0.0).astype(F32)[None, :]
    pos_b = jnp.broadcast_to(positions.reshape(T, 1), (T, LANES))
    tm = min(T, 1024)
    spec = pl.BlockSpec((tm, LANES), lambda i: (i, 0))
    return pl.pallas_call(
        _rope_table_kernel,
        grid=(T // tm,),
        in_specs=[spec, pl.BlockSpec((1, LANES), lambda i: (0, 0))],
        out_specs=[spec, spec, spec],
        out_shape=[jax.ShapeDtypeStruct((T, LANES), F32)] * 3,
        compiler_params=_params(("parallel",)),
        name="rope_tables",
    )(pos_b, freq)


ATT_OUT = 7 * LANES


def _in_proj_kernel(x_ref, gain_ref, watt_ref, wdqkv_ref, wdz_ref, wab_ref, wrw_ref,
                    c_ref, s1_ref, s2_ref,
                    att_ref, dqkv_ref, dz_ref, dab_ref, rwp_ref):
    x = x_ref[...]
    ms = jnp.mean(x * x, axis=-1, keepdims=True)
    h = (x * lax.rsqrt(ms + NORM_EPS) * gain_ref[...]).astype(BF16)
    att = _mm(h, watt_ref[...])
    c, s1, s2 = c_ref[...], s1_ref[...], s2_ref[...]
    half = ROPE_DIM // 2
    for slab in range(4):
        t = att[:, slab * LANES:(slab + 1) * LANES]
        t = t * c + pltpu.roll(t, LANES - half, 1) * s1 + pltpu.roll(t, half, 1) * s2
        if slab < 3:
            att_ref[:, slab * LANES:(slab + 1) * LANES] = (t * (HEAD_DIM ** -0.5)).astype(BF16)
        else:
            att_ref[:, 3 * LANES:4 * LANES] = t.astype(BF16)
            att_ref[:, 5 * LANES:6 * LANES] = pltpu.roll(t, HEAD_DIM, 1).astype(BF16)
    v = att[:, 4 * LANES:5 * LANES]
    att_ref[:, 4 * LANES:5 * LANES] = v.astype(BF16)
    att_ref[:, 6 * LANES:7 * LANES] = pltpu.roll(v, HEAD_DIM, 1).astype(BF16)
    dqkv_ref[...] = _mm(h, wdqkv_ref[...])
    dz_ref[...] = _mm(h, wdz_ref[...])
    dab_ref[...] = _mm(h, wab_ref[...])
    rwp_ref[...] = _mm(h, wrw_ref[...])


def _in_proj(x2, gain, w, tables, tm=512):
    T = x2.shape[0]
    tm = min(tm, T)
    row = lambda wdt: pl.BlockSpec((tm, wdt), lambda i: (i, 0))
    full = lambda a: pl.BlockSpec(a.shape, lambda i: (0,) * a.ndim)
    widths = (ATT_OUT, 3 * DN_WIDTH, DN_WIDTH, LANES, RW_IN)
    dts = (BF16, F32, F32, F32, F32)
    return pl.pallas_call(
        _in_proj_kernel,
        grid=(T // tm,),
        in_specs=[row(D_MODEL), full(gain)] + [full(a) for a in w] + [row(LANES)] * 3,
        out_specs=[row(wd) for wd in widths],
        out_shape=[jax.ShapeDtypeStruct((T, wd), dt) for wd, dt in zip(widths, dts)],
        compiler_params=_params(("parallel",)),
        name="in_proj",
    )(x2, gain, *w, *tables)


def _attn_kernel(sink_ref, q_ref, *refs, tq, seq):
    kv_refs = refs[:12]
    o_ref = refs[12]
    win = refs[13:17]
    i = pl.program_id(1)
    for a in range(4):
        prev, main, nxt = kv_refs[3 * a:3 * a + 3]
        win[a][0:WINDOW, :] = prev[0]
        win[a][WINDOW:WINDOW + tq, :] = main[0]
        win[a][WINDOW + tq:, :] = nxt[0]
    kA, vA, kB, vB = win
    blk = WINDOW
    lane_hi = _iota((3 * blk, LANES), 1) >= HEAD_DIM
    qlane_hi = _iota((blk, LANES), 1) >= HEAD_DIM
    qi = _iota((blk, 3 * blk), 0)
    kk = _iota((blk, 3 * blk), 1)
    rel = kk - blk - qi
    band = (rel <= WINDOW) & (rel >= -WINDOW)
    for s in range(tq // blk):
        r0 = s * blk
        abs_k = i * tq + r0 + kk - blk
        ok = band & (abs_k >= 0) & (abs_k < seq)
        kwin = {0: kA[r0:r0 + 3 * blk, :], 1: kB[r0:r0 + 3 * blk, :]}
        vwin = {0: vA[r0:r0 + 3 * blk, :], 1: vB[r0:r0 + 3 * blk, :]}
        for p in range(3):
            qp = q_ref[0, r0:r0 + blk, p * LANES:(p + 1) * LANES]
            acc = jnp.zeros((blk, LANES), F32)
            for half in range(2):
                h = 2 * p + half
                kvh = h // 3
                src = 0 if kvh == half else 1
                is_half = qlane_hi if half == 1 else jnp.logical_not(qlane_hi)
                qm = jnp.where(is_half, qp, jnp.zeros_like(qp))
                sc = _mm(qm, kwin[src], NT)
                sc = jnp.where(ok, sc, NEG_BIG)
                sk = sink_ref[h]
                m = jnp.maximum(jnp.max(sc, axis=-1, keepdims=True), sk)
                pexp = jnp.exp(sc - m)
                den = jnp.sum(pexp, axis=-1, keepdims=True) + jnp.exp(sk - m)
                pn = (pexp / den).astype(BF16)
                v_half = lane_hi if half == 1 else jnp.logical_not(lane_hi)
                vm = jnp.where(v_half, vwin[src], jnp.zeros_like(vwin[src]))
                acc = acc + _mm(pn, vm)
            o_ref[0, r0:r0 + blk, p * LANES:(p + 1) * LANES] = acc.astype(o_ref.dtype)


def _attention(att3, sink, tq=512):
    B, S, _ = att3.shape
    tq = min(tq, S)
    nb = tq // WINDOW
    last = S // WINDOW - 1
    specs = [pl.BlockSpec(memory_space=pltpu.SMEM),
             pl.BlockSpec((1, tq, ATT_WIDTH), lambda b, i: (b, i, 0))]
    for col in (3, 4, 5, 6):
        specs.append(pl.BlockSpec((1, WINDOW, LANES),
                                  lambda b, i, col=col: (b, jnp.maximum(i * nb - 1, 0), col)))
        specs.append(pl.BlockSpec((1, tq, LANES), lambda b, i, col=col: (b, i, col)))
        specs.append(pl.BlockSpec((1, WINDOW, LANES),
                                  lambda b, i, col=col: (b, jnp.minimum((i + 1) * nb, last), col)))
    return pl.pallas_call(
        functools.partial(_attn_kernel, tq=tq, seq=S),
        grid=(B, S // tq),
        in_specs=specs,
        out_specs=pl.BlockSpec((1, tq, ATT_WIDTH), lambda b, i: (b, i, 0)),
        out_shape=jax.ShapeDtypeStruct((B, S, ATT_WIDTH), BF16),
        scratch_shapes=[pltpu.VMEM((tq + 2 * WINDOW, LANES), BF16)] * 4,
        compiler_params=_params(("parallel", "parallel")),
        name="window_attention",
    )(sink, att3, *([att3] * 12))


def _chunk_consts(reverse):
    C = CHUNK
    ri = _iota((C, LANES), 0)
    li = _iota((C, LANES), 1)
    j = li & (HEAD_DIM - 1)
    rc = _iota((C, C), 0)
    cc = _iota((C, C), 1)
    r2 = _iota((LANES, LANES), 0)
    c2 = _iota((LANES, LANES), 1)
    if reverse:
        incl, strict, tm = j >= ri, j > ri, cc >= rc
    else:
        incl, strict, tm = j <= ri, j < ri, cc <= rc
    return dict(
        incl=incl, strict=strict, hi=li >= HEAD_DIM, eye2=(j == ri),
        tm=jnp.where(tm, 1.0, 0.0).astype(BF16),
        ones_cc=jnp.ones((C, C), BF16),
        bdmask=(r2 >= HEAD_DIM) == (c2 >= HEAD_DIM),
        eye=(r2 == c2),
        ones=jnp.ones((LANES, LANES), BF16),
        tot_row=0 if reverse else C - 1,
    )


def _bd(y):
    hi = (_iota(y.shape, 1) & (LANES - 1)) >= HEAD_DIM
    zero = jnp.zeros_like(y)
    return jnp.concatenate([jnp.where(hi, zero, y), jnp.where(hi, y, zero)], axis=0)


def _apply(x2, y):
    return _mm3(x2, _bd(y))


def _neumann(n2, y):
    y = y + _apply(n2, y)
    pw = n2
    steps = CHUNK.bit_length() - 2
    for _ in range(steps):
        pw = _apply(pw, pw)
        y = y + _apply(pw, y)
    return y


def _dn_chunk(q2, k2, v2, gates, sel_g, sel_b, P, cs):
    C = CHUNK
    gb = _mmx(gates, sel_g)
    bb = _mmx(gates, sel_b)
    gcb = _xmm(cs["tm"], gb)
    row = cs["tot_row"]
    gtot = jnp.broadcast_to(gcb[row:row + 1, :], (C, LANES))
    eg = jnp.exp(gcb)
    kb = k2 * bb
    vb = v2 * bb
    kbe = kb * eg
    qg = q2 * eg
    kg = k2 * jnp.exp(gtot - gcb)
    rj = _xmm(cs["ones_cc"], jnp.where(cs["eye2"], gcb, 0.0))
    dec = jnp.exp(jnp.where(cs["incl"], gcb - rj, NEG_BIG))
    s2 = _mm3(jnp.concatenate([kb, q2], axis=0), _bd(k2), NT)
    l2 = jnp.where(cs["strict"], s2[:C] * dec, 0.0)
    qk = s2[C:] * dec
    y = _neumann(-l2, jnp.concatenate([vb, kbe], axis=1))
    value, kcum = y[:, :LANES], y[:, LANES:]
    sp = _mm3(jnp.concatenate([kcum, qg], axis=0), P)
    vnew = value - sp[:C]
    o = sp[C:] + _apply(qk, vnew)
    glast = jnp.exp(jnp.concatenate([gtot, gtot], axis=0))
    upd = _mm3(kg, vnew, TN)
    pn = P * glast + jnp.where(cs["bdmask"], upd, 0.0)
    return o, pn


def _rw_chunk(r2, v2, kk2, lw2, kd2, b2, P, cs):
    C = CHUNK
    gcb = _xmm(cs["tm"], lw2)
    row = cs["tot_row"]
    gtot = jnp.broadcast_to(gcb[row:row + 1, :], (C, LANES))
    at = -kk2 * jnp.exp(gcb - lw2)
    eneg = jnp.exp(-gcb)
    bt = b2 * eneg
    kt = kd2 * eneg
    rt = r2 * jnp.exp(gcb)
    er = jnp.exp(gtot - gcb)
    bh = b2 * er
    kh = kd2 * er
    lhs = jnp.concatenate([at, rt], axis=0)
    sb = _mm3(lhs, _bd(bt), NT)
    sk = _mm3(lhs, _bd(kt), NT)
    aab = jnp.where(cs["strict"], sb[:C], 0.0)
    arb = jnp.where(cs["incl"], sb[C:], 0.0)
    aak = jnp.where(cs["strict"], sk[:C], 0.0)
    ark = jnp.where(cs["incl"], sk[C:], 0.0)
    y = _neumann(aab, jnp.concatenate([_apply(aak, v2), at], axis=1))
    vp, ap = y[:, :LANES], y[:, LANES:]
    sp = _mm3(jnp.concatenate([ap, rt], axis=0), P)
    u = sp[:C] + vp
    out = sp[C:] + _apply(arb, u) + _apply(ark, v2)
    wtot = jnp.exp(jnp.concatenate([gtot, gtot], axis=0))
    wcol = _mmx(jnp.where(cs["eye"], wtot, 0.0), cs["ones"])
    upd = _mm3(jnp.concatenate([bh, kh], axis=0), jnp.concatenate([u, v2], axis=0), TN)
    pn = P * wcol + jnp.where(cs["bdmask"], upd, 0.0)
    return out, pn


def _dn_prep_kernel(prev_ref, main_ref, next_ref, dab_ref, cw_ref, alog_ref, dt_ref,
                    qkv_ref, gates_ref, xbuf, *, tm):
    i = pl.program_id(1)
    n = pl.num_programs(1)
    halo = 8
    pad = (DN_CONV - 1) // 2
    xbuf[0:halo, :] = jnp.where(i == 0, 0.0, prev_ref[0])
    xbuf[halo:halo + tm, :] = main_ref[0]
    xbuf[halo + tm:, :] = jnp.where(i == n - 1, 0.0, next_ref[0])
    ones = _head_ones(LANES)
    for slab in range(3 * DN_WIDTH // LANES):
        cols = slice(slab * LANES, (slab + 1) * LANES)
        y = jnp.zeros((tm, LANES), F32)
        for j in range(DN_CONV):
            y = y + cw_ref[j:j + 1, cols] * xbuf[halo - pad + j:halo - pad + j + tm, cols]
        y = y * _sigmoid(y)
        if slab < 6:
            ss = _mmx(y * y, ones)
            y = y * lax.rsqrt(ss + 1e-6)
        if slab < 3:
            y = y * (HEAD_DIM ** -0.5)
        qkv_ref[0, :, cols] = y
    ab = dab_ref[0]
    lane = _iota(ab.shape, 1)
    g = -jnp.exp(alog_ref[...]) * _softplus(ab + dt_ref[...])
    gates_ref[0] = jnp.where(lane < 2 * DN_HEADS, g, _sigmoid(ab))


def _dn_prep(dqkv3, dab3, conv_w, a_log, dt_bias, tm=512):
    B, S, W = dqkv3.shape
    tm = min(tm, S)
    nb8 = tm // 8
    last8 = S // 8 - 1
    cw = jnp.pad(conv_w, ((0, 8 - DN_CONV), (0, 0)))
    pad_row = lambda a: jnp.pad(a.reshape(1, -1), ((0, 0), (0, LANES - a.size)))
    small = lambda a: pl.BlockSpec(a.shape, lambda b, i: (0,) * a.ndim)
    alog, dt = pad_row(a_log), pad_row(dt_bias)
    return pl.pallas_call(
        functools.partial(_dn_prep_kernel, tm=tm),
        grid=(B, S // tm),
        in_specs=[pl.BlockSpec((1, 8, W), lambda b, i: (b, jnp.maximum(i * nb8 - 1, 0), 0)),
                  pl.BlockSpec((1, tm, W), lambda b, i: (b, i, 0)),
                  pl.BlockSpec((1, 8, W), lambda b, i: (b, jnp.minimum((i + 1) * nb8, last8), 0)),
                  pl.BlockSpec((1, tm, LANES), lambda b, i: (b, i, 0)),
                  small(cw), small(alog), small(dt)],
        out_specs=[pl.BlockSpec((1, tm, W), lambda b, i: (b, i, 0)),
                   pl.BlockSpec((1, tm, LANES), lambda b, i: (b, i, 0))],
        out_shape=[jax.ShapeDtypeStruct((B, S, W), F32),
                   jax.ShapeDtypeStruct((B, S, LANES), F32)],
        scratch_shapes=[pltpu.VMEM((tm + 16, W), F32)],
        compiler_params=_params(("parallel", "parallel")),
        name="deltanet_prep",
    )(dqkv3, dqkv3, dqkv3, dab3, cw, alog, dt)


def _gate_select(lane_lo, lane_hi):
    r = _iota((LANES, LANES), 0)
    c = _iota((LANES, LANES), 1)
    src = jnp.where(c >= HEAD_DIM, lane_hi, lane_lo)
    return jnp.where(r == src, 1.0, 0.0).astype(BF16)


def _dn_scan_kernel(qf_ref, gf_ref, qb_ref, gb_ref, of_ref, ob_ref, state, *, cpb):
    n = pl.program_id(1)

    @pl.when(n == 0)
    def _():
        state[...] = jnp.zeros_like(state)

    pairs = DN_HEADS // 2
    for d, (q_ref, g_ref, o_ref) in enumerate(((qf_ref, gf_ref, of_ref), (qb_ref, gb_ref, ob_ref))):
        cs = _chunk_consts(reverse=(d == 1))
        order = range(cpb) if d == 0 else range(cpb - 1, -1, -1)
        for c in order:
            rows = slice(c * CHUNK, (c + 1) * CHUNK)
            gates = g_ref[0, rows, :]
            for p in range(pairs):
                base = d * DN_HEADS + 2 * p
                sel_g = _gate_select(base, base + 1)
                sel_b = _gate_select(2 * DN_HEADS + base, 2 * DN_HEADS + base + 1)
                q2 = q_ref[0, rows, p * LANES:(p + 1) * LANES]
                k2 = q_ref[0, rows, DN_WIDTH + p * LANES:DN_WIDTH + (p + 1) * LANES]
                v2 = q_ref[0, rows, 2 * DN_WIDTH + p * LANES:2 * DN_WIDTH + (p + 1) * LANES]
                o, pn = _dn_chunk(q2, k2, v2, gates, sel_g, sel_b, state[d, p], cs)
                state[d, p] = pn
                o_ref[0, rows, p * LANES:(p + 1) * LANES] = o


def _dn_scan(qkv3, gates3, cpb=2):
    B, S, W = qkv3.shape
    blk = cpb * CHUNK
    nblk = S // blk
    fwd = lambda b, n: (b, n, 0)
    bwd = lambda b, n: (b, nblk - 1 - n, 0)
    return pl.pallas_call(
        functools.partial(_dn_scan_kernel, cpb=cpb),
        grid=(B, nblk),
        in_specs=[pl.BlockSpec((1, blk, W), fwd), pl.BlockSpec((1, blk, LANES), fwd),
                  pl.BlockSpec((1, blk, W), bwd), pl.BlockSpec((1, blk, LANES), bwd)],
        out_specs=[pl.BlockSpec((1, blk, DN_WIDTH), fwd), pl.BlockSpec((1, blk, DN_WIDTH), bwd)],
        out_shape=[jax.ShapeDtypeStruct((B, S, DN_WIDTH), F32)] * 2,
        scratch_shapes=[pltpu.VMEM((2, DN_HEADS // 2, LANES, LANES), F32)],
        compiler_params=_params(("parallel", "arbitrary")),
        name="deltanet_scan",
    )(qkv3, gates3, qkv3, gates3)


RW_OUTS = 11


def _rw_prep_kernel(prev_ref, main_ref, next_ref, mu_ref, w0_ref, a0_ref, kk_ref, ka_ref, rk_ref,
                    wup_ref, aup_ref, gup_ref, *refs, tm):
    outs = refs[:RW_OUTS]
    xbuf, pbuf = refs[RW_OUTS:]
    i = pl.program_id(1)
    n = pl.num_programs(1)
    halo = 8
    xbuf[0:halo, :] = jnp.where(i == 0, 0.0, prev_ref[0])
    xbuf[halo:halo + tm, :] = main_ref[0]
    xbuf[halo + tm:, :] = jnp.where(i == n - 1, 0.0, next_ref[0])
    for slab in range(RW_IN // LANES):
        cols = slice(slab * LANES, (slab + 1) * LANES)
        cur = xbuf[halo:halo + tm, cols]
        prv = xbuf[halo - 1:halo - 1 + tm, cols]
        nxt = xbuf[halo + 1:halo + 1 + tm, cols]
        pbuf[:, cols] = cur + mu_ref[0:1, cols] * (prv - cur) + mu_ref[1:2, cols] * (nxt - cur)
    r = pbuf[:, 0:RW_WIDTH]
    k = pbuf[:, RW_WIDTH:2 * RW_WIDTH]
    v = pbuf[:, 2 * RW_WIDTH:3 * RW_WIDTH]
    low = pbuf[:, 3 * RW_WIDTH:3 * RW_WIDTH + LANES]
    gd = pbuf[:, 3 * RW_WIDTH + LANES:]
    ones = _head_ones(RW_WIDTH)
    r_o, v_o, kk_o, g_o, bg_o = outs[:5]
    g = _mm3(_sigmoid(gd), gup_ref[...])
    kq = k * kk_ref[...]
    kk = kq * lax.rsqrt(_mmx(kq * kq, ones) + 1e-6)
    bonus = _mmx(r * k * rk_ref[...], ones) * v
    r_o[0] = r
    v_o[0] = v
    kk_o[0] = kk
    g_o[0] = g
    bg_o[0] = bonus * g
    wl = jnp.tanh(low)
    for d in range(2):
        lw_o, kd_o, b_o = outs[5 + 3 * d:8 + 3 * d]
        w = -_softplus(-(w0_ref[d:d + 1, :] + _mm3(wl, wup_ref[d]))) - 0.5
        lw_o[0] = -jnp.exp(w)
        a = _sigmoid(a0_ref[d:d + 1, :] + _mm3(low, aup_ref[d]))
        kd_o[0] = k * (1.0 + (a - 1.0) * ka_ref[...])
        b_o[0] = kk * a


def _rw_prep(rwp3, mu, w0, w_up, a0, a_up, g_up, k_k, k_a, r_k, tm=256):
    B, S, W = rwp3.shape
    tm = min(tm, S)
    nb8 = tm // 8
    last8 = S // 8 - 1
    zeros = jnp.zeros((2, W_RANK, RW_WIDTH), F32)
    wup = jnp.concatenate([w_up, zeros], axis=1)
    aup = jnp.concatenate([zeros, a_up], axis=1)
    rowv = lambda a: a.reshape(1, RW_WIDTH)
    small = lambda a: pl.BlockSpec(a.shape, lambda b, i: (0,) * a.ndim)
    params = [mu, w0, a0, rowv(k_k), rowv(k_a), rowv(r_k), wup, aup, g_up]
    tile = pl.BlockSpec((1, tm, RW_WIDTH), lambda b, i: (b, i, 0))
    return pl.pallas_call(
        functools.partial(_rw_prep_kernel, tm=tm),
        grid=(B, S // tm),
        in_specs=[pl.BlockSpec((1, 8, W), lambda b, i: (b, jnp.maximum(i * nb8 - 1, 0), 0)),
                  pl.BlockSpec((1, tm, W), lambda b, i: (b, i, 0)),
                  pl.BlockSpec((1, 8, W), lambda b, i: (b, jnp.minimum((i + 1) * nb8, last8), 0))]
                 + [small(a) for a in params],
        out_specs=[tile] * RW_OUTS,
        out_shape=[jax.ShapeDtypeStruct((B, S, RW_WIDTH), F32)] * RW_OUTS,
        scratch_shapes=[pltpu.VMEM((tm + 16, W), F32), pltpu.VMEM((tm, W), F32)],
        compiler_params=_params(("parallel", "parallel")),
        name="rwkv_prep",
    )(rwp3, rwp3, rwp3, *params)


def _rw_scan_kernel(*refs, cpb):
    ins = refs[:12]
    yf_ref, yb_ref, state = refs[12:]
    n = pl.program_id(1)

    @pl.when(n == 0)
    def _():
        state[...] = jnp.zeros_like(state)

    pairs = RW_HEADS // 2
    for d, o_ref in enumerate((yf_ref, yb_ref)):
        cs = _chunk_consts(reverse=(d == 1))
        src = ins[6 * d:6 * d + 6]
        order = range(cpb) if d == 0 else range(cpb - 1, -1, -1)
        for c in order:
            rows = slice(c * CHUNK, (c + 1) * CHUNK)
            for p in range(pairs):
                cols = slice(p * LANES, (p + 1) * LANES)
                args = [ref[0, rows, cols] for ref in src]
                out, pn = _rw_chunk(*args, state[d, p], cs)
                state[d, p] = pn
                o_ref[0, rows, cols] = out


def _rw_scan(r, v, kk, dirs, cpb=2):
    B, S, W = r.shape
    blk = cpb * CHUNK
    nblk = S // blk
    fwd = pl.BlockSpec((1, blk, W), lambda b, n: (b, n, 0))
    bwd = pl.BlockSpec((1, blk, W), lambda b, n: (b, nblk - 1 - n, 0))
    return pl.pallas_call(
        functools.partial(_rw_scan_kernel, cpb=cpb),
        grid=(B, nblk),
        in_specs=[fwd] * 6 + [bwd] * 6,
        out_specs=[fwd, bwd],
        out_shape=[jax.ShapeDtypeStruct((B, S, W), F32)] * 2,
        scratch_shapes=[pltpu.VMEM((2, RW_HEADS // 2, LANES, LANES), F32)],
        compiler_params=_params(("parallel", "arbitrary")),
        name="rwkv_scan",
    )(r, v, kk, *dirs[0], r, v, kk, *dirs[1])


def _out_proj_kernel(x_ref, ya_ref, of_ref, ob_ref, z_ref, dnw_ref, yf_ref, yb_ref, g_ref, bg_ref,
                     lnw_ref, lnb_ref, wa_ref, wd_ref, wr_ref, gain_ref, xo_ref, h_ref):
    o = of_ref[...] + ob_ref[...]
    ms = _mmx(o * o, _head_ones(DN_WIDTH)) * (1.0 / HEAD_DIM)
    z = z_ref[...]
    ydn = o * lax.rsqrt(ms + NORM_EPS) * dnw_ref[...] * (z * _sigmoid(z))
    y = yf_ref[...] + yb_ref[...]
    ones = _head_ones(RW_WIDTH)
    mean = _mmx(y, ones) * (1.0 / HEAD_DIM)
    yc = y - mean
    var = _mmx(yc * yc, ones) * (1.0 / HEAD_DIM)
    yrw = (yc * lax.rsqrt(var + RW_LN_EPS) * lnw_ref[...] + lnb_ref[...]) * g_ref[...] + bg_ref[...]
    acc = _mm(ya_ref[...], wa_ref[...])
    acc = acc + _mm(ydn.astype(BF16), wd_ref[...])
    acc = acc + _mm(yrw.astype(BF16), wr_ref[...])
    xn = x_ref[...] + acc
    xo_ref[...] = xn
    ms2 = jnp.mean(xn * xn, axis=-1, keepdims=True)
    h_ref[...] = (xn * lax.rsqrt(ms2 + NORM_EPS) * gain_ref[...]).astype(BF16)


def _out_proj(x2, ya, of, ob, z, dnw, yf, yb, g, bg, lnw, lnb, w_out, gain, tm=512):
    T = x2.shape[0]
    tm = min(tm, T)
    row = lambda a: pl.BlockSpec((tm, a.shape[1]), lambda i: (i, 0))
    full = lambda a: pl.BlockSpec(a.shape, lambda i: (0,) * a.ndim)
    wa = w_out[:ATT_WIDTH].astype(BF16)
    wd = w_out[ATT_WIDTH:ATT_WIDTH + DN_WIDTH].astype(BF16)
    wr = w_out[ATT_WIDTH + DN_WIDTH:].astype(BF16)
    args = [x2, ya, of, ob, z, dnw, yf, yb, g, bg, lnw, lnb, wa, wd, wr, gain]
    is_row = [True, True, True, True, True, False, True, True, True, True, False, False,
              False, False, False, False]
    return pl.pallas_call(
        _out_proj_kernel,
        grid=(T // tm,),
        in_specs=[row(a) if r else full(a) for a, r in zip(args, is_row)],
        out_specs=[pl.BlockSpec((tm, D_MODEL), lambda i: (i, 0))] * 2,
        out_shape=[jax.ShapeDtypeStruct((T, D_MODEL), F32), jax.ShapeDtypeStruct((T, D_MODEL), BF16)],
        compiler_params=_params(("parallel",)),
        name="out_proj",
    )(*args)


FFN_HALO = 16


def _ffn_kernel(prev_ref, main_ref, next_ref, x_ref, wg_ref, wv_ref, cg_ref, cv_ref, wd_ref, fin_ref,
                o_ref, hbuf, acc, *, tm, final_norm):
    i = pl.program_id(1)
    j = pl.program_id(2)
    ni = pl.num_programs(1)
    nj = pl.num_programs(2)

    @pl.when(j == 0)
    def _():
        zero = jnp.zeros((FFN_HALO, D_MODEL), BF16)
        hbuf[0:FFN_HALO, :] = jnp.where(i == 0, zero, prev_ref[0])
        hbuf[FFN_HALO:FFN_HALO + tm, :] = main_ref[0]
        hbuf[FFN_HALO + tm:, :] = jnp.where(i == ni - 1, zero, next_ref[0])
        acc[...] = jnp.zeros_like(acc)

    h = hbuf[...]
    rows = tm + 2 * FFN_HALO

    def conv(u, c_ref):
        um = pltpu.roll(u, 1, 0)
        up = pltpu.roll(u, rows - 1, 0)
        y = c_ref[0:1, :] * um + c_ref[1:2, :] * u + c_ref[2:3, :] * up
        return y[FFN_HALO:FFN_HALO + tm]

    gate = conv(_mm(h, wg_ref[...]), cg_ref)
    val = conv(_mm(h, wv_ref[...]), cv_ref)
    act = (gate * _sigmoid(gate) * val).astype(BF16)
    acc[...] += _mm(act, wd_ref[...])

    @pl.when(j == nj - 1)
    def _():
        xn = x_ref[0] + acc[...]
        if final_norm:
            ms = jnp.mean(xn * xn, axis=-1, keepdims=True)
            xn = xn * lax.rsqrt(ms + NORM_EPS) * fin_ref[...]
        o_ref[0] = xn


def _ffn(h3, x3, w_up, conv_w, w_down, fin_gain, final_norm, tm=512, tf=256):
    B, S, _ = h3.shape
    tm = min(tm, S)
    nbh = tm // FFN_HALO
    lasth = S // FFN_HALO - 1
    nj = D_FF // tf
    wup = w_up.astype(BF16)
    wdn = w_down.astype(BF16)
    cw = jnp.pad(conv_w, ((0, 8 - conv_w.shape[0]), (0, 0)))
    return pl.pallas_call(
        functools.partial(_ffn_kernel, tm=tm, final_norm=final_norm),
        grid=(B, S // tm, nj),
        in_specs=[
            pl.BlockSpec((1, FFN_HALO, D_MODEL), lambda b, i, j: (b, jnp.maximum(i * nbh - 1, 0), 0)),
            pl.BlockSpec((1, tm, D_MODEL), lambda b, i, j: (b, i, 0)),
            pl.BlockSpec((1, FFN_HALO, D_MODEL),
                         lambda b, i, j: (b, jnp.minimum((i + 1) * nbh, lasth), 0)),
            pl.BlockSpec((1, tm, D_MODEL), lambda b, i, j: (b, i, 0)),
            pl.BlockSpec((D_MODEL, tf), lambda b, i, j: (0, j)),
            pl.BlockSpec((D_MODEL, tf), lambda b, i, j: (0, nj + j)),
            pl.BlockSpec((8, tf), lambda b, i, j: (0, j)),
            pl.BlockSpec((8, tf), lambda b, i, j: (0, nj + j)),
            pl.BlockSpec((tf, D_MODEL), lambda b, i, j: (j, 0)),
            pl.BlockSpec((1, D_MODEL), lambda b, i, j: (0, 0)),
        ],
        out_specs=pl.BlockSpec((1, tm, D_MODEL), lambda b, i, j: (b, i, 0)),
        out_shape=jax.ShapeDtypeStruct((B, S, D_MODEL), F32),
        scratch_shapes=[pltpu.VMEM((tm + 2 * FFN_HALO, D_MODEL), BF16),
                        pltpu.VMEM((tm, D_MODEL), F32)],
        compiler_params=_params(("parallel", "parallel", "arbitrary")),
        name="conv_ffn",
    )(h3, h3, h3, x3, wup, wup, cw, cw, wdn, fin_gain)


def _in_weights(w_in_l):
    a = ATT_WIDTH + 2 * ATT_KV_WIDTH
    d = a + 3 * DN_WIDTH
    z = d + DN_WIDTH
    g = z + 4 * DN_HEADS
    w_ab = jnp.pad(w_in_l[:, z:g], ((0, 0), (0, LANES - 4 * DN_HEADS)))
    groups = (w_in_l[:, :a], w_in_l[:, a:d], w_in_l[:, d:z], w_ab, w_in_l[:, g:])
    return [w.astype(BF16) for w in groups]


def kernel(x, positions, norm_mix, w_in, attn_sink, dn_conv, dn_a_log, dn_dt_bias, dn_norm, rw_mu, rw_w0, rw_w_up, rw_a0, rw_a_up, rw_g_up, rw_k_k, rw_k_a, rw_r_k, rw_ln_w, rw_ln_b, w_out, norm_ffn, ffn_w_up, ffn_conv, ffn_w_down, norm_final):
    B, S, D = x.shape
    T = B * S
    depth = w_in.shape[0]
    tables = _rope_tables(positions)
    x2 = x.reshape(T, D)
    for l in range(depth):
        att, dqkv, dz, dab, rwp = _in_proj(x2, norm_mix[l].reshape(1, D), _in_weights(w_in[l]), tables)
        y_att = _attention(att.reshape(B, S, ATT_OUT), attn_sink[l])
        qkv_act, gates = _dn_prep(dqkv.reshape(B, S, -1), dab.reshape(B, S, LANES),
                                  dn_conv[l], dn_a_log[l], dn_dt_bias[l])
        o_f, o_b = _dn_scan(qkv_act, gates)
        rw = _rw_prep(rwp.reshape(B, S, RW_IN), rw_mu[l], rw_w0[l], rw_w_up[l], rw_a0[l], rw_a_up[l],
                      rw_g_up[l], rw_k_k[l], rw_k_a[l], rw_r_k[l])
        r, v, kk, g, bg = rw[:5]
        y_f, y_b = _rw_scan(r, v, kk, (rw[5:8], rw[8:11]))
        flat = lambda a: a.reshape(T, a.shape[-1])
        x2, h2 = _out_proj(
            x2, flat(y_att), flat(o_f), flat(o_b), dz,
            jnp.tile(dn_norm[l], DN_HEADS).reshape(1, DN_WIDTH),
            flat(y_f), flat(y_b), flat(g), flat(bg),
            rw_ln_w[l].reshape(1, RW_WIDTH), rw_ln_b[l].reshape(1, RW_WIDTH),
            w_out[l], norm_ffn[l].reshape(1, D))
        x2 = _ffn(h2.reshape(B, S, D), x2.reshape(B, S, D), ffn_w_up[l], ffn_conv[l], ffn_w_down[l],
                  norm_final.reshape(1, D), final_norm=(l == depth - 1)).reshape(T, D)
    return x2.reshape(B, S, D)
```

```python
import functools

import jax
import jax.numpy as jnp
from jax import lax
from jax.experimental import pallas as pl
from jax.experimental.pallas import tpu as pltpu

F32 = jnp.float32
BF16 = jnp.bfloat16

D_MODEL = 1024
HEAD_DIM = 64
HEAD_SHIFT = 6
ATT_HEADS = 6
ATT_WIDTH = 384
ATT_KV_WIDTH = 128
WINDOW = 128
ROPE_DIM = 16
ROPE_THETA = 500000.0
DN_HEADS = 6
DN_WIDTH = 384
DN_CONV = 5
RW_HEADS = 4
RW_WIDTH = 256
W_RANK = 64
A_RANK = 64
G_RANK = 128
RW_IN = 1024
D_FF = 2816
NORM_EPS = 1e-6
RW_LN_EPS = 64e-5
LANES = 128
CHUNK = 64
NEG_BIG = -1e30

NN = (((1,), (0,)), ((), ()))
NT = (((1,), (1,)), ((), ()))
TN = (((0,), (0,)), ((), ()))

VMEM_LIMIT = 56 * 1024 * 1024


def _mm(a, b, dims=NN):
    return lax.dot_general(a, b, dims, preferred_element_type=F32)


def _split2(x):
    hi = x.astype(BF16)
    lo = (x - hi.astype(F32)).astype(BF16)
    return hi, lo


def _split3(x):
    hi = x.astype(BF16)
    r = x - hi.astype(F32)
    mid = r.astype(BF16)
    lo = (r - mid.astype(F32)).astype(BF16)
    return hi, mid, lo


def _mm3(a, b, dims=NN):
    ah, al = _split2(a)
    bh, bl = _split2(b)
    return _mm(ah, bh, dims) + (_mm(ah, bl, dims) + _mm(al, bh, dims))


def _mmx(a, e, dims=NN):
    h, m, l = _split3(a)
    return _mm(h, e, dims) + (_mm(m, e, dims) + _mm(l, e, dims))


def _xmm(e, a, dims=NN):
    h, m, l = _split3(a)
    return _mm(e, h, dims) + (_mm(e, m, dims) + _mm(e, l, dims))


def _iota(shape, dim):
    return lax.broadcasted_iota(jnp.int32, shape, dim)


def _sigmoid(x):
    return 1.0 / (1.0 + jnp.exp(-x))


def _softplus(x):
    return jnp.maximum(x, 0.0) + jnp.log1p(jnp.exp(-jnp.abs(x)))


def _head_ones(width):
    r = _iota((width, width), 0) >> HEAD_SHIFT
    c = _iota((width, width), 1) >> HEAD_SHIFT
    return jnp.where(r == c, 1.0, 0.0).astype(BF16)


def _params(sem, vmem=VMEM_LIMIT):
    return pltpu.CompilerParams(dimension_semantics=sem, vmem_limit_bytes=vmem)


def _rope_table_kernel(pos_ref, freq_ref, c_ref, s1_ref, s2_ref):
    ang = pos_ref[...].astype(F32) * freq_ref[...]
    c = jnp.cos(ang)
    s = jnp.sin(ang)
    j = _iota(ang.shape, 1) & (HEAD_DIM - 1)
    half = ROPE_DIM // 2
    c_ref[...] = c
    s1_ref[...] = jnp.where(j < half, -s, 0.0)
    s2_ref[...] = jnp.where((j >= half) & (j < ROPE_DIM), s, 0.0)


def _rope_tables(positions):
    T = positions.size
    half = ROPE_DIM // 2
    inv_freq = ROPE_THETA ** (-jnp.arange(half, dtype=F32) / half)
    lane = jnp.arange(LANES)
    freq = jnp.where((lane % HEAD_DIM) < ROPE_DIM, inv_freq[lane % half], 0.0).astype(F32)[None, :]
    pos_b = jnp.broadcast_to(positions.reshape(T, 1), (T, LANES))
    tm = min(T, 1024)
    spec = pl.BlockSpec((tm, LANES), lambda i: (i, 0))
    return pl.pallas_call(
        _rope_table_kernel,
        grid=(T // tm,),
        in_specs=[spec, pl.BlockSpec((1, LANES), lambda i: (0, 0))],
        out_specs=[spec, spec, spec],
        out_shape=[jax.ShapeDtypeStruct((T, LANES), F32)] * 3,
        compiler_params=_params(("parallel",)),
        name="rope_tables",
    )(pos_b, freq)


ATT_OUT = 7 * LANES


def _in_proj_kernel(x_ref, gain_ref, watt_ref, wdqkv_ref, wdz_ref, wab_ref, wrw_ref,
                    c_ref, s1_ref, s2_ref,
                    att_ref, dqkv_ref, dz_ref, dab_ref, rwp_ref):
    x = x_ref[...]
    ms = jnp.mean(x * x, axis=-1, keepdims=True)
    h = (x * lax.rsqrt(ms + NORM_EPS) * gain_ref[...]).astype(BF16)
    att = _mm(h, watt_ref[...])
    c, s1, s2 = c_ref[...], s1_ref[...], s2_ref[...]
    half = ROPE_DIM // 2
    for slab in range(4):
        t = att[:, slab * LANES:(slab + 1) * LANES]
        t = t * c + pltpu.roll(t, LANES - half, 1) * s1 + pltpu.roll(t, half, 1) * s2
        if slab < 3:
            att_ref[:, slab * LANES:(slab + 1) * LANES] = (t * (HEAD_DIM ** -0.5)).astype(BF16)
        else:
            att_ref[:, 3 * LANES:4 * LANES] = t.astype(BF16)
            att_ref[:, 5 * LANES:6 * LANES] = pltpu.roll(t, HEAD_DIM, 1).astype(BF16)
    v = att[:, 4 * LANES:5 * LANES]
    att_ref[:, 4 * LANES:5 * LANES] = v.astype(BF16)
    att_ref[:, 6 * LANES:7 * LANES] = pltpu.roll(v, HEAD_DIM, 1).astype(BF16)
    dqkv_ref[...] = _mm(h, wdqkv_ref[...])
    dz_ref[...] = _mm(h, wdz_ref[...])
    dab_ref[...] = _mm(h, wab_ref[...])
    rwp_ref[...] = _mm(h, wrw_ref[...])


def _in_proj(x2, gain, w, tables, tm=512):
    T = x2.shape[0]
    tm = min(tm, T)
    row = lambda wdt: pl.BlockSpec((tm, wdt), lambda i: (i, 0))
    full = lambda a: pl.BlockSpec(a.shape, lambda i: (0,) * a.ndim)
    widths = (ATT_OUT, 3 * DN_WIDTH, DN_WIDTH, LANES, RW_IN)
    dts = (BF16, F32, F32, F32, F32)
    return pl.pallas_call(
        _in_proj_kernel,
        grid=(T // tm,),
        in_specs=[row(D_MODEL), full(gain)] + [full(a) for a in w] + [row(LANES)] * 3,
        out_specs=[row(wd) for wd in widths],
        out_shape=[jax.ShapeDtypeStruct((T, wd), dt) for wd, dt in zip(widths, dts)],
        compiler_params=_params(("parallel",)),
        name="in_proj",
    )(x2, gain, *w, *tables)


def _attn_kernel(sink_ref, q_ref, *refs, tq, seq):
    kv_refs = refs[:12]
    o_ref = refs[12]
    win = refs[13:17]
    i = pl.program_id(1)
    for a in range(4):
        prev, main, nxt = kv_refs[3 * a:3 * a + 3]
        win[a][0:WINDOW, :] = prev[0]
        win[a][WINDOW:WINDOW + tq, :] = main[0]
        win[a][WINDOW + tq:, :] = nxt[0]
    kA, vA, kB, vB = win
    blk = WINDOW
    lane_hi = _iota((3 * blk, LANES), 1) >= HEAD_DIM
    qlane_hi = _iota((blk, LANES), 1) >= HEAD_DIM
    qi = _iota((blk, 3 * blk), 0)
    kk = _iota((blk, 3 * blk), 1)
    rel = kk - blk - qi
    band = (rel <= WINDOW) & (rel >= -WINDOW)
    for s in range(tq // blk):
        r0 = s * blk
        abs_k = i * tq + r0 + kk - blk
        ok = band & (abs_k >= 0) & (abs_k < seq)
        kwin = {0: kA[r0:r0 + 3 * blk, :], 1: kB[r0:r0 + 3 * blk, :]}
        vwin = {0: vA[r0:r0 + 3 * blk, :], 1: vB[r0:r0 + 3 * blk, :]}
        for p in range(3):
            qp = q_ref[0, r0:r0 + blk, p * LANES:(p + 1) * LANES]
            acc = jnp.zeros((blk, LANES), F32)
            for half in range(2):
                h = 2 * p + half
                kvh = h // 3
                src = 0 if kvh == half else 1
                is_half = qlane_hi if half == 1 else jnp.logical_not(qlane_hi)
                qm = jnp.where(is_half, qp, jnp.zeros_like(qp))
                sc = _mm(qm, kwin[src], NT)
                sc = jnp.where(ok, sc, NEG_BIG)
                sk = sink_ref[h]
                m = jnp.maximum(jnp.max(sc, axis=-1, keepdims=True), sk)
                pexp = jnp.exp(sc - m)
                den = jnp.sum(pexp, axis=-1, keepdims=True) + jnp.exp(sk - m)
                pn = (pexp / den).astype(BF16)
                v_half = lane_hi if half == 1 else jnp.logical_not(lane_hi)
                vm = jnp.where(v_half, vwin[src], jnp.zeros_like(vwin[src]))
                acc = acc + _mm(pn, vm)
            o_ref[0, r0:r0 + blk, p * LANES:(p + 1) * LANES] = acc.astype(o_ref.dtype)


def _attention(att3, sink, tq=512):
    B, S, _ = att3.shape
    tq = min(tq, S)
    nb = tq // WINDOW
    last = S // WINDOW - 1
    specs = [pl.BlockSpec(memory_space=pltpu.SMEM),
             pl.BlockSpec((1, tq, ATT_WIDTH), lambda b, i: (b, i, 0))]
    for col in (3, 4, 5, 6):
        specs.append(pl.BlockSpec((1, WINDOW, LANES),
                                  lambda b, i, col=col: (b, jnp.maximum(i * nb - 1, 0), col)))
        specs.append(pl.BlockSpec((1, tq, LANES), lambda b, i, col=col: (b, i, col)))
        specs.append(pl.BlockSpec((1, WINDOW, LANES),
                                  lambda b, i, col=col: (b, jnp.minimum((i + 1) * nb, last), col)))
    return pl.pallas_call(
        functools.partial(_attn_kernel, tq=tq, seq=S),
        grid=(B, S // tq),
        in_specs=specs,
        out_specs=pl.BlockSpec((1, tq, ATT_WIDTH), lambda b, i: (b, i, 0)),
        out_shape=jax.ShapeDtypeStruct((B, S, ATT_WIDTH), BF16),
        scratch_shapes=[pltpu.VMEM((tq + 2 * WINDOW, LANES), BF16)] * 4,
        compiler_params=_params(("parallel", "parallel")),
        name="window_attention",
    )(sink, att3, *([att3] * 12))


def _chunk_consts(reverse):
    C = CHUNK
    ri = _iota((C, LANES), 0)
    li = _iota((C, LANES), 1)
    j = li & (HEAD_DIM - 1)
    rc = _iota((C, C), 0)
    cc = _iota((C, C), 1)
    r2 = _iota((LANES, LANES), 0)
    c2 = _iota((LANES, LANES), 1)
    if reverse:
        incl, strict, tm = j >= ri, j > ri, cc >= rc
    else:
        incl, strict, tm = j <= ri, j < ri, cc <= rc
    return dict(
        incl=incl, strict=strict, hi=li >= HEAD_DIM, eye2=(j == ri),
        tm=jnp.where(tm, 1.0, 0.0).astype(BF16),
        ones_cc=jnp.ones((C, C), BF16),
        bdmask=(r2 >= HEAD_DIM) == (c2 >= HEAD_DIM),
        eye=(r2 == c2),
        ones=jnp.ones((LANES, LANES), BF16),
        tot_row=0 if reverse else C - 1,
    )


def _bd(y):
    hi = (_iota(y.shape, 1) & (LANES - 1)) >= HEAD_DIM
    zero = jnp.zeros_like(y)
    return jnp.concatenate([jnp.where(hi, zero, y), jnp.where(hi, y, zero)], axis=0)


def _apply(x2, y):
    return _mm3(x2, _bd(y))


_NEED_STATE = object()


def _neumann(n2, y):
    pw = n2
    levels = CHUNK.bit_length() - 1
    for lvl in range(levels):
        y_next = y + _apply(pw, y)
        if lvl + 1 < levels:
            pw = _apply(pw, pw)
        y = y_next
        yield
    return y


def _interleave(gens, first_state, pred):
    n = len(gens)
    results = [None] * n
    waiting = [False] * n
    done = [False] * n
    while not all(done):
        for i in range(n):
            if done[i]:
                continue
            try:
                if waiting[i]:
                    if pred[i] is None:
                        state = first_state[i]
                    elif done[pred[i]]:
                        state = results[pred[i]][1]
                    else:
                        continue
                    waiting[i] = False
                    token = gens[i].send(state)
                else:
                    token = next(gens[i])
                if token is _NEED_STATE:
                    waiting[i] = True
            except StopIteration as stop:
                results[i] = stop.value
                done[i] = True
    return results


def _dn_chunk(q2, k2, v2, gates, sel_g, sel_b, cs):
    C = CHUNK
    gb = _mmx(gates, sel_g)
    bb = _mmx(gates, sel_b)
    yield
    gcb = _xmm(cs["tm"], gb)
    kb = k2 * bb
    vb = v2 * bb
    s2 = _mm3(jnp.concatenate([kb, q2], axis=0), _bd(k2), NT)
    yield
    row = cs["tot_row"]
    gtot = jnp.broadcast_to(gcb[row:row + 1, :], (C, LANES))
    eg = jnp.exp(gcb)
    kbe = kb * eg
    qg = q2 * eg
    kg = k2 * jnp.exp(gtot - gcb)
    glast = jnp.exp(jnp.concatenate([gtot, gtot], axis=0))
    rj = _xmm(cs["ones_cc"], jnp.where(cs["eye2"], gcb, 0.0))
    yield
    dec = jnp.exp(jnp.where(cs["incl"], gcb - rj, NEG_BIG))
    l2 = jnp.where(cs["strict"], s2[:C] * dec, 0.0)
    qk = s2[C:] * dec
    y = yield from _neumann(-l2, jnp.concatenate([vb, kbe], axis=1))
    value, kcum = y[:, :LANES], y[:, LANES:]
    P = yield _NEED_STATE
    sp = _mm3(jnp.concatenate([kcum, qg], axis=0), P)
    yield
    vnew = value - sp[:C]
    o = sp[C:] + _apply(qk, vnew)
    upd = _mm3(kg, vnew, TN)
    yield
    pn = P * glast + jnp.where(cs["bdmask"], upd, 0.0)
    return o, pn


def _rw_chunk(r2, v2, kk2, lw2, kd2, b2, cs):
    C = CHUNK
    gcb = _xmm(cs["tm"], lw2)
    yield
    row = cs["tot_row"]
    gtot = jnp.broadcast_to(gcb[row:row + 1, :], (C, LANES))
    at = -kk2 * jnp.exp(gcb - lw2)
    eneg = jnp.exp(-gcb)
    bt = b2 * eneg
    kt = kd2 * eneg
    rt = r2 * jnp.exp(gcb)
    er = jnp.exp(gtot - gcb)
    bh = b2 * er
    kh = kd2 * er
    lhs = jnp.concatenate([at, rt], axis=0)
    sb = _mm3(lhs, _bd(bt), NT)
    sk = _mm3(lhs, _bd(kt), NT)
    wtot = jnp.exp(jnp.concatenate([gtot, gtot], axis=0))
    wcol = _mmx(jnp.where(cs["eye"], wtot, 0.0), cs["ones"])
    yield
    aab = jnp.where(cs["strict"], sb[:C], 0.0)
    arb = jnp.where(cs["incl"], sb[C:], 0.0)
    aak = jnp.where(cs["strict"], sk[:C], 0.0)
    ark = jnp.where(cs["incl"], sk[C:], 0.0)
    av = _apply(aak, v2)
    arkv = _apply(ark, v2)
    yield
    y = yield from _neumann(aab, jnp.concatenate([av, at], axis=1))
    vp, ap = y[:, :LANES], y[:, LANES:]
    P = yield _NEED_STATE
    sp = _mm3(jnp.concatenate([ap, rt], axis=0), P)
    yield
    u = sp[:C] + vp
    out = sp[C:] + _apply(arb, u) + arkv
    upd = _mm3(jnp.concatenate([bh, kh], axis=0), jnp.concatenate([u, v2], axis=0), TN)
    yield
    pn = P * wcol + jnp.where(cs["bdmask"], upd, 0.0)
    return out, pn


def _dn_prep_kernel(prev_ref, main_ref, next_ref, dab_ref, cw_ref, alog_ref, dt_ref,
                    qkv_ref, gates_ref, xbuf, *, tm):
    i = pl.program_id(1)
    n = pl.num_programs(1)
    halo = 8
    pad = (DN_CONV - 1) // 2
    xbuf[0:halo, :] = jnp.where(i == 0, 0.0, prev_ref[0])
    xbuf[halo:halo + tm, :] = main_ref[0]
    xbuf[halo + tm:, :] = jnp.where(i == n - 1, 0.0, next_ref[0])
    ones = _head_ones(LANES)
    for slab in range(3 * DN_WIDTH // LANES):
        cols = slice(slab * LANES, (slab + 1) * LANES)
        y = jnp.zeros((tm, LANES), F32)
        for j in range(DN_CONV):
            y = y + cw_ref[j:j + 1, cols] * xbuf[halo - pad + j:halo - pad + j + tm, cols]
        y = y * _sigmoid(y)
        if slab < 6:
            ss = _mmx(y * y, ones)
            y = y * lax.rsqrt(ss + 1e-6)
        if slab < 3:
            y = y * (HEAD_DIM ** -0.5)
        qkv_ref[0, :, cols] = y
    ab = dab_ref[0]
    lane = _iota(ab.shape, 1)
    g = -jnp.exp(alog_ref[...]) * _softplus(ab + dt_ref[...])
    gates_ref[0] = jnp.where(lane < 2 * DN_HEADS, g, _sigmoid(ab))


def _dn_prep(dqkv3, dab3, conv_w, a_log, dt_bias, tm=512):
    B, S, W = dqkv3.shape
    tm = min(tm, S)
    nb8 = tm // 8
    last8 = S // 8 - 1
    cw = jnp.pad(conv_w, ((0, 8 - DN_CONV), (0, 0)))
    pad_row = lambda a: jnp.pad(a.reshape(1, -1), ((0, 0), (0, LANES - a.size)))
    small = lambda a: pl.BlockSpec(a.shape, lambda b, i: (0,) * a.ndim)
    alog, dt = pad_row(a_log), pad_row(dt_bias)
    return pl.pallas_call(
        functools.partial(_dn_prep_kernel, tm=tm),
        grid=(B, S // tm),
        in_specs=[pl.BlockSpec((1, 8, W), lambda b, i: (b, jnp.maximum(i * nb8 - 1, 0), 0)),
                  pl.BlockSpec((1, tm, W), lambda b, i: (b, i, 0)),
                  pl.BlockSpec((1, 8, W), lambda b, i: (b, jnp.minimum((i + 1) * nb8, last8), 0)),
                  pl.BlockSpec((1, tm, LANES), lambda b, i: (b, i, 0)),
                  small(cw), small(alog), small(dt)],
        out_specs=[pl.BlockSpec((1, tm, W), lambda b, i: (b, i, 0)),
                   pl.BlockSpec((1, tm, LANES), lambda b, i: (b, i, 0))],
        out_shape=[jax.ShapeDtypeStruct((B, S, W), F32),
                   jax.ShapeDtypeStruct((B, S, LANES), F32)],
        scratch_shapes=[pltpu.VMEM((tm + 16, W), F32)],
        compiler_params=_params(("parallel", "parallel")),
        name="deltanet_prep",
    )(dqkv3, dqkv3, dqkv3, dab3, cw, alog, dt)


def _gate_select(lane_lo, lane_hi):
    r = _iota((LANES, LANES), 0)
    c = _iota((LANES, LANES), 1)
    src = jnp.where(c >= HEAD_DIM, lane_hi, lane_lo)
    return jnp.where(r == src, 1.0, 0.0).astype(BF16)


def _dn_scan_kernel(qf_ref, gf_ref, qb_ref, gb_ref, of_ref, ob_ref, state, *, cpb):
    n = pl.program_id(1)

    @pl.when(n == 0)
    def _():
        state[...] = jnp.zeros_like(state)

    pairs = DN_HEADS // 2
    gens, first, pred, dest = [], [], [], []
    last = {}
    consts = [_chunk_consts(reverse=False), _chunk_consts(reverse=True)]
    for step in range(cpb):
        for d, (q_ref, g_ref, o_ref) in enumerate(((qf_ref, gf_ref, of_ref), (qb_ref, gb_ref, ob_ref))):
            cs = consts[d]
            c = step if d == 0 else cpb - 1 - step
            rows = slice(c * CHUNK, (c + 1) * CHUNK)
            gates = g_ref[0, rows, :]
            for p in range(pairs):
                base = d * DN_HEADS + 2 * p
                sel_g = _gate_select(base, base + 1)
                sel_b = _gate_select(2 * DN_HEADS + base, 2 * DN_HEADS + base + 1)
                q2 = q_ref[0, rows, p * LANES:(p + 1) * LANES]
                k2 = q_ref[0, rows, DN_WIDTH + p * LANES:DN_WIDTH + (p + 1) * LANES]
                v2 = q_ref[0, rows, 2 * DN_WIDTH + p * LANES:2 * DN_WIDTH + (p + 1) * LANES]
                pred.append(last.get((d, p)))
                first.append(state[d, p] if step == 0 else None)
                last[(d, p)] = len(gens)
                dest.append((o_ref, rows, p))
                gens.append(_dn_chunk(q2, k2, v2, gates, sel_g, sel_b, cs))
    results = _interleave(gens, first, pred)
    for (o_ref, rows, p), (o, _) in zip(dest, results):
        o_ref[0, rows, p * LANES:(p + 1) * LANES] = o
    for (d, p), idx in last.items():
        state[d, p] = results[idx][1]


def _dn_scan(qkv3, gates3, cpb=2):
    B, S, W = qkv3.shape
    blk = cpb * CHUNK
    nblk = S // blk
    fwd = lambda b, n: (b, n, 0)
    bwd = lambda b, n: (b, nblk - 1 - n, 0)
    return pl.pallas_call(
        functools.partial(_dn_scan_kernel, cpb=cpb),
        grid=(B, nblk),
        in_specs=[pl.BlockSpec((1, blk, W), fwd), pl.BlockSpec((1, blk, LANES), fwd),
                  pl.BlockSpec((1, blk, W), bwd), pl.BlockSpec((1, blk, LANES), bwd)],
        out_specs=[pl.BlockSpec((1, blk, DN_WIDTH), fwd), pl.BlockSpec((1, blk, DN_WIDTH), bwd)],
        out_shape=[jax.ShapeDtypeStruct((B, S, DN_WIDTH), F32)] * 2,
        scratch_shapes=[pltpu.VMEM((2, DN_HEADS // 2, LANES, LANES), F32)],
        compiler_params=_params(("parallel", "arbitrary")),
        name="deltanet_scan",
    )(qkv3, gates3, qkv3, gates3)


RW_OUTS = 11


def _rw_prep_kernel(prev_ref, main_ref, next_ref, mu_ref, w0_ref, a0_ref, kk_ref, ka_ref, rk_ref,
                    wup_ref, aup_ref, gup_ref, *refs, tm):
    outs = refs[:RW_OUTS]
    xbuf, pbuf = refs[RW_OUTS:]
    i = pl.program_id(1)
    n = pl.num_programs(1)
    halo = 8
    xbuf[0:halo, :] = jnp.where(i == 0, 0.0, prev_ref[0])
    xbuf[halo:halo + tm, :] = main_ref[0]
    xbuf[halo + tm:, :] = jnp.where(i == n - 1, 0.0, next_ref[0])
    for slab in range(RW_IN // LANES):
        cols = slice(slab * LANES, (slab + 1) * LANES)
        cur = xbuf[halo:halo + tm, cols]
        prv = xbuf[halo - 1:halo - 1 + tm, cols]
        nxt = xbuf[halo + 1:halo + 1 + tm, cols]
        pbuf[:, cols] = cur + mu_ref[0:1, cols] * (prv - cur) + mu_ref[1:2, cols] * (nxt - cur)
    r = pbuf[:, 0:RW_WIDTH]
    k = pbuf[:, RW_WIDTH:2 * RW_WIDTH]
    v = pbuf[:, 2 * RW_WIDTH:3 * RW_WIDTH]
    low = pbuf[:, 3 * RW_WIDTH:3 * RW_WIDTH + LANES]
    gd = pbuf[:, 3 * RW_WIDTH + LANES:]
    ones = _head_ones(RW_WIDTH)
    r_o, v_o, kk_o, g_o, bg_o = outs[:5]
    g = _mm3(_sigmoid(gd), gup_ref[...])
    kq = k * kk_ref[...]
    kk = kq * lax.rsqrt(_mmx(kq * kq, ones) + 1e-6)
    bonus = _mmx(r * k * rk_ref[...], ones) * v
    r_o[0] = r
    v_o[0] = v
    kk_o[0] = kk
    g_o[0] = g
    bg_o[0] = bonus * g
    wl = jnp.tanh(low)
    for d in range(2):
        lw_o, kd_o, b_o = outs[5 + 3 * d:8 + 3 * d]
        w = -_softplus(-(w0_ref[d:d + 1, :] + _mm3(wl, wup_ref[d]))) - 0.5
        lw_o[0] = -jnp.exp(w)
        a = _sigmoid(a0_ref[d:d + 1, :] + _mm3(low, aup_ref[d]))
        kd_o[0] = k * (1.0 + (a - 1.0) * ka_ref[...])
        b_o[0] = kk * a


def _rw_prep(rwp3, mu, w0, w_up, a0, a_up, g_up, k_k, k_a, r_k, tm=256):
    B, S, W = rwp3.shape
    tm = min(tm, S)
    nb8 = tm // 8
    last8 = S // 8 - 1
    zeros = jnp.zeros((2, W_RANK, RW_WIDTH), F32)
    wup = jnp.concatenate([w_up, zeros], axis=1)
    aup = jnp.concatenate([zeros, a_up], axis=1)
    rowv = lambda a: a.reshape(1, RW_WIDTH)
    small = lambda a: pl.BlockSpec(a.shape, lambda b, i: (0,) * a.ndim)
    params = [mu, w0, a0, rowv(k_k), rowv(k_a), rowv(r_k), wup, aup, g_up]
    tile = pl.BlockSpec((1, tm, RW_WIDTH), lambda b, i: (b, i, 0))
    return pl.pallas_call(
        functools.partial(_rw_prep_kernel, tm=tm),
        grid=(B, S // tm),
        in_specs=[pl.BlockSpec((1, 8, W), lambda b, i: (b, jnp.maximum(i * nb8 - 1, 0), 0)),
                  pl.BlockSpec((1, tm, W), lambda b, i: (b, i, 0)),
                  pl.BlockSpec((1, 8, W), lambda b, i: (b, jnp.minimum((i + 1) * nb8, last8), 0))]
                 + [small(a) for a in params],
        out_specs=[tile] * RW_OUTS,
        out_shape=[jax.ShapeDtypeStruct((B, S, RW_WIDTH), F32)] * RW_OUTS,
        scratch_shapes=[pltpu.VMEM((tm + 16, W), F32), pltpu.VMEM((tm, W), F32)],
        compiler_params=_params(("parallel", "parallel")),
        name="rwkv_prep",
    )(rwp3, rwp3, rwp3, *params)


def _rw_scan_kernel(*refs, cpb):
    ins = refs[:12]
    yf_ref, yb_ref, state = refs[12:]
    n = pl.program_id(1)

    @pl.when(n == 0)
    def _():
        state[...] = jnp.zeros_like(state)

    pairs = RW_HEADS // 2
    gens, first, pred, dest = [], [], [], []
    last = {}
    consts = [_chunk_consts(reverse=False), _chunk_consts(reverse=True)]
    for step in range(cpb):
        for d, o_ref in enumerate((yf_ref, yb_ref)):
            cs = consts[d]
            src = ins[6 * d:6 * d + 6]
            c = step if d == 0 else cpb - 1 - step
            rows = slice(c * CHUNK, (c + 1) * CHUNK)
            for p in range(pairs):
                cols = slice(p * LANES, (p + 1) * LANES)
                args = [ref[0, rows, cols] for ref in src]
                pred.append(last.get((d, p)))
                first.append(state[d, p] if step == 0 else None)
                last[(d, p)] = len(gens)
                dest.append((o_ref, rows, cols))
                gens.append(_rw_chunk(*args, cs))
    results = _interleave(gens, first, pred)
    for (o_ref, rows, cols), (out, _) in zip(dest, results):
        o_ref[0, rows, cols] = out
    for (d, p), idx in last.items():
        state[d, p] = results[idx][1]


def _rw_scan(r, v, kk, dirs, cpb=2):
    B, S, W = r.shape
    blk = cpb * CHUNK
    nblk = S // blk
    fwd = pl.BlockSpec((1, blk, W), lambda b, n: (b, n, 0))
    bwd = pl.BlockSpec((1, blk, W), lambda b, n: (b, nblk - 1 - n, 0))
    return pl.pallas_call(
        functools.partial(_rw_scan_kernel, cpb=cpb),
        grid=(B, nblk),
        in_specs=[fwd] * 6 + [bwd] * 6,
        out_specs=[fwd, bwd],
        out_shape=[jax.ShapeDtypeStruct((B, S, W), F32)] * 2,
        scratch_shapes=[pltpu.VMEM((2, RW_HEADS // 2, LANES, LANES), F32)],
        compiler_params=_params(("parallel", "arbitrary")),
        name="rwkv_scan",
    )(r, v, kk, *dirs[0], r, v, kk, *dirs[1])


def _out_proj_kernel(x_ref, ya_ref, of_ref, ob_ref, z_ref, dnw_ref, yf_ref, yb_ref, g_ref, bg_ref,
                     lnw_ref, lnb_ref, wa_ref, wd_ref, wr_ref, gain_ref, xo_ref, h_ref):
    o = of_ref[...] + ob_ref[...]
    ms = _mmx(o * o, _head_ones(DN_WIDTH)) * (1.0 / HEAD_DIM)
    z = z_ref[...]
    ydn = o * lax.rsqrt(ms + NORM_EPS) * dnw_ref[...] * (z * _sigmoid(z))
    y = yf_ref[...] + yb_ref[...]
    ones = _head_ones(RW_WIDTH)
    mean = _mmx(y, ones) * (1.0 / HEAD_DIM)
    yc = y - mean
    var = _mmx(yc * yc, ones) * (1.0 / HEAD_DIM)
    yrw = (yc * lax.rsqrt(var + RW_LN_EPS) * lnw_ref[...] + lnb_ref[...]) * g_ref[...] + bg_ref[...]
    acc = _mm(ya_ref[...], wa_ref[...])
    acc = acc + _mm(ydn.astype(BF16), wd_ref[...])
    acc = acc + _mm(yrw.astype(BF16), wr_ref[...])
    xn = x_ref[...] + acc
    xo_ref[...] = xn
    ms2 = jnp.mean(xn * xn, axis=-1, keepdims=True)
    h_ref[...] = (xn * lax.rsqrt(ms2 + NORM_EPS) * gain_ref[...]).astype(BF16)


def _out_proj(x2, ya, of, ob, z, dnw, yf, yb, g, bg, lnw, lnb, w_out, gain, tm=512):
    T = x2.shape[0]
    tm = min(tm, T)
    row = lambda a: pl.BlockSpec((tm, a.shape[1]), lambda i: (i, 0))
    full = lambda a: pl.BlockSpec(a.shape, lambda i: (0,) * a.ndim)
    wa = w_out[:ATT_WIDTH].astype(BF16)
    wd = w_out[ATT_WIDTH:ATT_WIDTH + DN_WIDTH].astype(BF16)
    wr = w_out[ATT_WIDTH + DN_WIDTH:].astype(BF16)
    args = [x2, ya, of, ob, z, dnw, yf, yb, g, bg, lnw, lnb, wa, wd, wr, gain]
    is_row = [True, True, True, True, True, False, True, True, True, True, False, False,
              False, False, False, False]
    return pl.pallas_call(
        _out_proj_kernel,
        grid=(T // tm,),
        in_specs=[row(a) if r else full(a) for a, r in zip(args, is_row)],
        out_specs=[pl.BlockSpec((tm, D_MODEL), lambda i: (i, 0))] * 2,
        out_shape=[jax.ShapeDtypeStruct((T, D_MODEL), F32), jax.ShapeDtypeStruct((T, D_MODEL), BF16)],
        compiler_params=_params(("parallel",)),
        name="out_proj",
    )(*args)


FFN_HALO = 16


def _ffn_kernel(prev_ref, main_ref, next_ref, x_ref, wg_ref, wv_ref, cg_ref, cv_ref, wd_ref, fin_ref,
                o_ref, hbuf, acc, *, tm, final_norm):
    i = pl.program_id(1)
    j = pl.program_id(2)
    ni = pl.num_programs(1)
    nj = pl.num_programs(2)

    @pl.when(j == 0)
    def _():
        zero = jnp.zeros((FFN_HALO, D_MODEL), BF16)
        hbuf[0:FFN_HALO, :] = jnp.where(i == 0, zero, prev_ref[0])
        hbuf[FFN_HALO:FFN_HALO + tm, :] = main_ref[0]
        hbuf[FFN_HALO + tm:, :] = jnp.where(i == ni - 1, zero, next_ref[0])
        acc[...] = jnp.zeros_like(acc)

    h = hbuf[...]
    rows = tm + 2 * FFN_HALO

    def conv(u, c_ref):
        um = pltpu.roll(u, 1, 0)
        up = pltpu.roll(u, rows - 1, 0)
        y = c_ref[0:1, :] * um + c_ref[1:2, :] * u + c_ref[2:3, :] * up
        return y[FFN_HALO:FFN_HALO + tm]

    gate = conv(_mm(h, wg_ref[...]), cg_ref)
    val = conv(_mm(h, wv_ref[...]), cv_ref)
    act = (gate * _sigmoid(gate) * val).astype(BF16)
    acc[...] += _mm(act, wd_ref[...])

    @pl.when(j == nj - 1)
    def _():
        xn = x_ref[0] + acc[...]
        if final_norm:
            ms = jnp.mean(xn * xn, axis=-1, keepdims=True)
            xn = xn * lax.rsqrt(ms + NORM_EPS) * fin_ref[...]
        o_ref[0] = xn


def _ffn(h3, x3, w_up, conv_w, w_down, fin_gain, final_norm, tm=512, tf=256):
    B, S, _ = h3.shape
    tm = min(tm, S)
    nbh = tm // FFN_HALO
    lasth = S // FFN_HALO - 1
    nj = D_FF // tf
    wup = w_up.astype(BF16)
    wdn = w_down.astype(BF16)
    cw = jnp.pad(conv_w, ((0, 8 - conv_w.shape[0]), (0, 0)))
    return pl.pallas_call(
        functools.partial(_ffn_kernel, tm=tm, final_norm=final_norm),
        grid=(B, S // tm, nj),
        in_specs=[
            pl.BlockSpec((1, FFN_HALO, D_MODEL), lambda b, i, j: (b, jnp.maximum(i * nbh - 1, 0), 0)),
            pl.BlockSpec((1, tm, D_MODEL), lambda b, i, j: (b, i, 0)),
            pl.BlockSpec((1, FFN_HALO, D_MODEL),
                         lambda b, i, j: (b, jnp.minimum((i + 1) * nbh, lasth), 0)),
            pl.BlockSpec((1, tm, D_MODEL), lambda b, i, j: (b, i, 0)),
            pl.BlockSpec((D_MODEL, tf), lambda b, i, j: (0, j)),
            pl.BlockSpec((D_MODEL, tf), lambda b, i, j: (0, nj + j)),
            pl.BlockSpec((8, tf), lambda b, i, j: (0, j)),
            pl.BlockSpec((8, tf), lambda b, i, j: (0, nj + j)),
            pl.BlockSpec((tf, D_MODEL), lambda b, i, j: (j, 0)),
            pl.BlockSpec((1, D_MODEL), lambda b, i, j: (0, 0)),
        ],
        out_specs=pl.BlockSpec((1, tm, D_MODEL), lambda b, i, j: (b, i, 0)),
        out_shape=jax.ShapeDtypeStruct((B, S, D_MODEL), F32),
        scratch_shapes=[pltpu.VMEM((tm + 2 * FFN_HALO, D_MODEL), BF16),
                        pltpu.VMEM((tm, D_MODEL), F32)],
        compiler_params=_params(("parallel", "parallel", "arbitrary")),
        name="conv_ffn",
    )(h3, h3, h3, x3, wup, wup, cw, cw, wdn, fin_gain)


def _in_weights(w_in_l):
    a = ATT_WIDTH + 2 * ATT_KV_WIDTH
    d = a + 3 * DN_WIDTH
    z = d + DN_WIDTH
    g = z + 4 * DN_HEADS
    w_ab = jnp.pad(w_in_l[:, z:g], ((0, 0), (0, LANES - 4 * DN_HEADS)))
    groups = (w_in_l[:, :a], w_in_l[:, a:d], w_in_l[:, d:z], w_ab, w_in_l[:, g:])
    return [w.astype(BF16) for w in groups]


def kernel(x, positions, norm_mix, w_in, attn_sink, dn_conv, dn_a_log, dn_dt_bias, dn_norm, rw_mu, rw_w0, rw_w_up, rw_a0, rw_a_up, rw_g_up, rw_k_k, rw_k_a, rw_r_k, rw_ln_w, rw_ln_b, w_out, norm_ffn, ffn_w_up, ffn_conv, ffn_w_down, norm_final):
    B, S, D = x.shape
    T = B * S
    depth = w_in.shape[0]
    tables = _rope_tables(positions)
    x2 = x.reshape(T, D)
    for l in range(depth):
        att, dqkv, dz, dab, rwp = _in_proj(x2, norm_mix[l].reshape(1, D), _in_weights(w_in[l]), tables)
        y_att = _attention(att.reshape(B, S, ATT_OUT), attn_sink[l])
        qkv_act, gates = _dn_prep(dqkv.reshape(B, S, -1), dab.reshape(B, S, LANES),
                                  dn_conv[l], dn_a_log[l], dn_dt_bias[l])
        o_f, o_b = _dn_scan(qkv_act, gates)
        rw = _rw_prep(rwp.reshape(B, S, RW_IN), rw_mu[l], rw_w0[l], rw_w_up[l], rw_a0[l], rw_a_up[l],
                      rw_g_up[l], rw_k_k[l], rw_k_a[l], rw_r_k[l])
        r, v, kk, g, bg = rw[:5]
        y_f, y_b = _rw_scan(r, v, kk, (rw[5:8], rw[8:11]))
        flat = lambda a: a.reshape(T, a.shape[-1])
        x2, h2 = _out_proj(
            x2, flat(y_att), flat(o_f), flat(o_b), dz,
            jnp.tile(dn_norm[l], DN_HEADS).reshape(1, DN_WIDTH),
            flat(y_f), flat(y_b), flat(g), flat(bg),
            rw_ln_w[l].reshape(1, RW_WIDTH), rw_ln_b[l].reshape(1, RW_WIDTH),
            w_out[l], norm_ffn[l].reshape(1, D))
        x2 = _ffn(h2.reshape(B, S, D), x2.reshape(B, S, D), ffn_w_up[l], ffn_conv[l], ffn_w_down[l],
                  norm_final.reshape(1, D), final_norm=(l == depth - 1)).reshape(T, D)
    return x2.reshape(B, S, D)
```

```python
import functools

import jax
import jax.numpy as jnp
from jax import lax
from jax.experimental import pallas as pl
from jax.experimental.pallas import tpu as pltpu

F32 = jnp.float32
BF16 = jnp.bfloat16

D_MODEL = 1024
HEAD_DIM = 64
HEAD_SHIFT = 6
ATT_HEADS = 6
ATT_WIDTH = 384
ATT_KV_WIDTH = 128
WINDOW = 128
ROPE_DIM = 16
ROPE_THETA = 500000.0
DN_HEADS = 6
DN_WIDTH = 384
DN_CONV = 5
RW_HEADS = 4
RW_WIDTH = 256
W_RANK = 64
A_RANK = 64
G_RANK = 128
RW_IN = 1024
D_FF = 2816
NORM_EPS = 1e-6
RW_LN_EPS = 64e-5
LANES = 128
CHUNK = 64
NEG_BIG = -1e30

NN = (((1,), (0,)), ((), ()))
NT = (((1,), (1,)), ((), ()))
TN = (((0,), (0,)), ((), ()))

VMEM_LIMIT = 56 * 1024 * 1024


def _mm(a, b, dims=NN):
    return lax.dot_general(a, b, dims, preferred_element_type=F32)


def _split2(x):
    hi = x.astype(BF16)
    lo = (x - hi.astype(F32)).astype(BF16)
    return hi, lo


def _split3(x):
    hi = x.astype(BF16)
    r = x - hi.astype(F32)
    mid = r.astype(BF16)
    lo = (r - mid.astype(F32)).astype(BF16)
    return hi, mid, lo


def _mm3(a, b, dims=NN):
    ah, al = _split2(a)
    bh, bl = _split2(b)
    return _mm(ah, bh, dims) + (_mm(ah, bl, dims) + _mm(al, bh, dims))


def _mm1(a, b, dims=NN):
    return _mm(a.astype(BF16), b.astype(BF16), dims)


_mm_score = _mm1
_mm_neu_y = _mm1
_mm_neu_sq = _mm1
_mm_intra = _mm1
_mm_state = _mm1


def _mmx(a, e, dims=NN):
    h, m, l = _split3(a)
    return _mm(h, e, dims) + (_mm(m, e, dims) + _mm(l, e, dims))


def _xmm(e, a, dims=NN):
    h, m, l = _split3(a)
    return _mm(e, h, dims) + (_mm(e, m, dims) + _mm(e, l, dims))


def _iota(shape, dim):
    return lax.broadcasted_iota(jnp.int32, shape, dim)


def _sigmoid(x):
    return 1.0 / (1.0 + jnp.exp(-x))


def _softplus(x):
    return jnp.maximum(x, 0.0) + jnp.log1p(jnp.exp(-jnp.abs(x)))


def _head_ones(width):
    r = _iota((width, width), 0) >> HEAD_SHIFT
    c = _iota((width, width), 1) >> HEAD_SHIFT
    return jnp.where(r == c, 1.0, 0.0).astype(BF16)


def _params(sem, vmem=VMEM_LIMIT):
    return pltpu.CompilerParams(dimension_semantics=sem, vmem_limit_bytes=vmem)


def _rope_table_kernel(pos_ref, freq_ref, c_ref, s1_ref, s2_ref):
    ang = pos_ref[...].astype(F32) * freq_ref[...]
    c = jnp.cos(ang)
    s = jnp.sin(ang)
    j = _iota(ang.shape, 1) & (HEAD_DIM - 1)
    half = ROPE_DIM // 2
    c_ref[...] = c
    s1_ref[...] = jnp.where(j < half, -s, 0.0)
    s2_ref[...] = jnp.where((j >= half) & (j < ROPE_DIM), s, 0.0)


def _rope_tables(positions):
    T = positions.size
    half = ROPE_DIM // 2
    inv_freq = ROPE_THETA ** (-jnp.arange(half, dtype=F32) / half)
    lane = jnp.arange(LANES)
    freq = jnp.where((lane % HEAD_DIM) < ROPE_DIM, inv_freq[lane % half], 0.0).astype(F32)[None, :]
    pos_b = jnp.broadcast_to(positions.reshape(T, 1), (T, LANES))
    tm = min(T, 1024)
    spec = pl.BlockSpec((tm, LANES), lambda i: (i, 0))
    return pl.pallas_call(
        _rope_table_kernel,
        grid=(T // tm,),
        in_specs=[spec, pl.BlockSpec((1, LANES), lambda i: (0, 0))],
        out_specs=[spec, spec, spec],
        out_shape=[jax.ShapeDtypeStruct((T, LANES), F32)] * 3,
        compiler_params=_params(("parallel",)),
        name="rope_tables",
    )(pos_b, freq)


ATT_OUT = 7 * LANES


def _in_proj_kernel(x_ref, gain_ref, watt_ref, wdqkv_ref, wdz_ref, wab_ref, wrw_ref,
                    c_ref, s1_ref, s2_ref,
                    att_ref, dqkv_ref, dz_ref, dab_ref, rwp_ref):
    x = x_ref[...]
    ms = jnp.mean(x * x, axis=-1, keepdims=True)
    h = (x * lax.rsqrt(ms + NORM_EPS) * gain_ref[...]).astype(BF16)
    att = _mm(h, watt_ref[...])
    c, s1, s2 = c_ref[...], s1_ref[...], s2_ref[...]
    half = ROPE_DIM // 2
    for slab in range(4):
        t = att[:, slab * LANES:(slab + 1) * LANES]
        t = t * c + pltpu.roll(t, LANES - half, 1) * s1 + pltpu.roll(t, half, 1) * s2
        if slab < 3:
            att_ref[:, slab * LANES:(slab + 1) * LANES] = (t * (HEAD_DIM ** -0.5)).astype(BF16)
        else:
            att_ref[:, 3 * LANES:4 * LANES] = t.astype(BF16)
            att_ref[:, 5 * LANES:6 * LANES] = pltpu.roll(t, HEAD_DIM, 1).astype(BF16)
    v = att[:, 4 * LANES:5 * LANES]
    att_ref[:, 4 * LANES:5 * LANES] = v.astype(BF16)
    att_ref[:, 6 * LANES:7 * LANES] = pltpu.roll(v, HEAD_DIM, 1).astype(BF16)
    dqkv_ref[...] = _mm(h, wdqkv_ref[...])
    dz_ref[...] = _mm(h, wdz_ref[...])
    dab_ref[...] = _mm(h, wab_ref[...])
    rwp_ref[...] = _mm(h, wrw_ref[...])


def _in_proj(x2, gain, w, tables, tm=512):
    T = x2.shape[0]
    tm = min(tm, T)
    row = lambda wdt: pl.BlockSpec((tm, wdt), lambda i: (i, 0))
    full = lambda a: pl.BlockSpec(a.shape, lambda i: (0,) * a.ndim)
    widths = (ATT_OUT, 3 * DN_WIDTH, DN_WIDTH, LANES, RW_IN)
    dts = (BF16, F32, F32, F32, F32)
    return pl.pallas_call(
        _in_proj_kernel,
        grid=(T // tm,),
        in_specs=[row(D_MODEL), full(gain)] + [full(a) for a in w] + [row(LANES)] * 3,
        out_specs=[row(wd) for wd in widths],
        out_shape=[jax.ShapeDtypeStruct((T, wd), dt) for wd, dt in zip(widths, dts)],
        compiler_params=_params(("parallel",)),
        name="in_proj",
    )(x2, gain, *w, *tables)


def _attn_kernel(sink_ref, q_ref, *refs, tq, seq):
    kv_refs = refs[:12]
    o_ref = refs[12]
    win = refs[13:17]
    i = pl.program_id(1)
    for a in range(4):
        prev, main, nxt = kv_refs[3 * a:3 * a + 3]
        win[a][0:WINDOW, :] = prev[0]
        win[a][WINDOW:WINDOW + tq, :] = main[0]
        win[a][WINDOW + tq:, :] = nxt[0]
    kA, vA, kB, vB = win
    blk = WINDOW
    lane_hi = _iota((3 * blk, LANES), 1) >= HEAD_DIM
    qlane_hi = _iota((blk, LANES), 1) >= HEAD_DIM
    qi = _iota((blk, 3 * blk), 0)
    kk = _iota((blk, 3 * blk), 1)
    rel = kk - blk - qi
    band = (rel <= WINDOW) & (rel >= -WINDOW)
    for s in range(tq // blk):
        r0 = s * blk
        abs_k = i * tq + r0 + kk - blk
        ok = band & (abs_k >= 0) & (abs_k < seq)
        kwin = {0: kA[r0:r0 + 3 * blk, :], 1: kB[r0:r0 + 3 * blk, :]}
        vwin = {0: vA[r0:r0 + 3 * blk, :], 1: vB[r0:r0 + 3 * blk, :]}
        src = [0 if (h // 3) == (h % 2) else 1 for h in range(ATT_HEADS)]
        scores = []
        for h in range(ATT_HEADS):
            qp = q_ref[0, r0:r0 + blk, (h // 2) * LANES:(h // 2 + 1) * LANES]
            is_half = qlane_hi if h % 2 == 1 else jnp.logical_not(qlane_hi)
            qm = jnp.where(is_half, qp, jnp.zeros_like(qp))
            scores.append(_mm(qm, kwin[src[h]], NT))
        probs = []
        for h in range(ATT_HEADS):
            sc = jnp.where(ok, scores[h], NEG_BIG)
            sk = sink_ref[h]
            m = jnp.maximum(jnp.max(sc, axis=-1, keepdims=True), sk)
            pexp = jnp.exp(sc - m)
            den = jnp.sum(pexp, axis=-1, keepdims=True) + jnp.exp(sk - m)
            probs.append((pexp / den).astype(BF16))
        for p in range(ATT_HEADS // 2):
            acc = jnp.zeros((blk, LANES), F32)
            for half in range(2):
                h = 2 * p + half
                v_half = lane_hi if half == 1 else jnp.logical_not(lane_hi)
                vm = jnp.where(v_half, vwin[src[h]], jnp.zeros_like(vwin[src[h]]))
                acc = acc + _mm(probs[h], vm)
            o_ref[0, r0:r0 + blk, p * LANES:(p + 1) * LANES] = acc.astype(o_ref.dtype)


def _attention(att3, sink, tq=512):
    B, S, _ = att3.shape
    tq = min(tq, S)
    nb = tq // WINDOW
    last = S // WINDOW - 1
    specs = [pl.BlockSpec(memory_space=pltpu.SMEM),
             pl.BlockSpec((1, tq, ATT_WIDTH), lambda b, i: (b, i, 0))]
    for col in (3, 4, 5, 6):
        specs.append(pl.BlockSpec((1, WINDOW, LANES),
                                  lambda b, i, col=col: (b, jnp.maximum(i * nb - 1, 0), col)))
        specs.append(pl.BlockSpec((1, tq, LANES), lambda b, i, col=col: (b, i, col)))
        specs.append(pl.BlockSpec((1, WINDOW, LANES),
                                  lambda b, i, col=col: (b, jnp.minimum((i + 1) * nb, last), col)))
    return pl.pallas_call(
        functools.partial(_attn_kernel, tq=tq, seq=S),
        grid=(B, S // tq),
        in_specs=specs,
        out_specs=pl.BlockSpec((1, tq, ATT_WIDTH), lambda b, i: (b, i, 0)),
        out_shape=jax.ShapeDtypeStruct((B, S, ATT_WIDTH), BF16),
        scratch_shapes=[pltpu.VMEM((tq + 2 * WINDOW, LANES), BF16)] * 4,
        compiler_params=_params(("parallel", "parallel")),
        name="window_attention",
    )(sink, att3, *([att3] * 12))


def _cumsum_mats():
    rc = _iota((CHUNK, CHUNK), 0)
    cc = _iota((CHUNK, CHUNK), 1)
    return (jnp.where(cc <= rc, 1.0, 0.0).astype(BF16), jnp.where(cc >= rc, 1.0, 0.0).astype(BF16))


def _chunk_consts(reverse):
    C = CHUNK
    ri = _iota((C, LANES), 0)
    li = _iota((C, LANES), 1)
    j = li & (HEAD_DIM - 1)
    r2 = _iota((LANES, LANES), 0)
    c2 = _iota((LANES, LANES), 1)
    if reverse:
        incl, strict = j >= ri, j > ri
    else:
        incl, strict = j <= ri, j < ri
    return dict(
        incl=incl, strict=strict, hi=li >= HEAD_DIM,
        bdmask=(r2 >= HEAD_DIM) == (c2 >= HEAD_DIM),
        tot_row=0 if reverse else C - 1,
    )


def _bd(y):
    hi = (_iota(y.shape, 1) & (LANES - 1)) >= HEAD_DIM
    zero = jnp.zeros_like(y)
    return jnp.concatenate([jnp.where(hi, zero, y), jnp.where(hi, y, zero)], axis=0)


def _apply(x2, y, mm):
    return mm(x2, _bd(y))


_NEED_STATE = object()


def _neumann(n2, y):
    eye = (_iota(n2.shape, 1) & (HEAD_DIM - 1)) == _iota(n2.shape, 0)
    t = n2 + jnp.where(eye, 1.0, 0.0)
    pw = _apply(n2, n2, _mm_neu_sq)
    yield
    levels = CHUNK.bit_length() - 2
    for _ in range(levels):
        both = _apply(pw, jnp.concatenate([t, pw], axis=1), _mm_neu_sq)
        t = t + both[:, :LANES]
        pw = both[:, LANES:]
        yield
    return _apply(t, y, _mm_neu_y)


def _interleave(gens, first_state, pred):
    n = len(gens)
    results = [None] * n
    waiting = [False] * n
    done = [False] * n
    while not all(done):
        for i in range(n):
            if done[i]:
                continue
            try:
                if waiting[i]:
                    if pred[i] is None:
                        state = first_state[i]
                    elif done[pred[i]]:
                        state = results[pred[i]][1]
                    else:
                        continue
                    waiting[i] = False
                    token = gens[i].send(state)
                else:
                    token = next(gens[i])
                if token is _NEED_STATE:
                    waiting[i] = True
            except StopIteration as stop:
                results[i] = stop.value
                done[i] = True
    return results


def _dn_chunk(q2, k2, v2, gates, sel, gc_row, cs):
    C = CHUNK
    gx = _mmx(gates, sel)
    yield
    gcb, bb = gx[:, :LANES], gx[:, LANES:]
    kb = k2 * bb
    vb = v2 * bb
    s2 = _mm_score(jnp.concatenate([kb, q2], axis=0), _bd(k2), NT)
    row = cs["tot_row"]
    gtot = jnp.broadcast_to(gcb[row:row + 1, :], (C, LANES))
    eg = jnp.exp(gcb)
    kbe = kb * eg
    qg = q2 * eg
    kg = k2 * jnp.exp(gtot - gcb)
    glast = jnp.exp(jnp.concatenate([gtot, gtot], axis=0))
    rj = jnp.broadcast_to(gc_row, (C, LANES))
    yield
    dec = jnp.exp(jnp.where(cs["incl"], gcb - rj, NEG_BIG))
    l2 = jnp.where(cs["strict"], s2[:C] * dec, 0.0)
    qk = s2[C:] * dec
    y = yield from _neumann(-l2, jnp.concatenate([vb, kbe], axis=1))
    value, kcum = y[:, :LANES], y[:, LANES:]
    P = yield _NEED_STATE
    sp = _mm_state(jnp.concatenate([kcum, qg], axis=0), P)
    yield
    vnew = value - sp[:C]
    o = sp[C:] + _apply(qk, vnew, _mm_state)
    upd = _mm_state(kg, vnew, TN)
    yield
    pn = P * glast + jnp.where(cs["bdmask"], upd, 0.0)
    return o, pn


def _rw_chunk(r2, v2, at, gcb, kd2, b2, cs):
    C = CHUNK
    row = cs["tot_row"]
    gtot = jnp.broadcast_to(gcb[row:row + 1, :], (C, LANES))
    eneg = jnp.exp(-gcb)
    bt = b2 * eneg
    kt = kd2 * eneg
    rt = r2 * jnp.exp(gcb)
    er = jnp.exp(gtot - gcb)
    bh = b2 * er
    kh = kd2 * er
    lhs = jnp.concatenate([at, rt], axis=0)
    sb = _mm_score(lhs, _bd(bt), NT)
    sk = _mm_score(lhs, _bd(kt), NT)
    wtot = jnp.exp(jnp.concatenate([gtot, gtot], axis=0))
    wcol = wtot.T
    yield
    aab = jnp.where(cs["strict"], sb[:C], 0.0)
    arb = jnp.where(cs["incl"], sb[C:], 0.0)
    aak = jnp.where(cs["strict"], sk[:C], 0.0)
    ark = jnp.where(cs["incl"], sk[C:], 0.0)
    av = _apply(aak, v2, _mm_intra)
    arkv = _apply(ark, v2, _mm_intra)
    yield
    y = yield from _neumann(aab, jnp.concatenate([av, at], axis=1))
    vp, ap = y[:, :LANES], y[:, LANES:]
    P = yield _NEED_STATE
    sp = _mm_state(jnp.concatenate([ap, rt], axis=0), P)
    yield
    u = sp[:C] + vp
    out = sp[C:] + _apply(arb, u, _mm_state) + arkv
    upd = _mm_state(jnp.concatenate([bh, kh], axis=0), jnp.concatenate([u, v2], axis=0), TN)
    yield
    pn = P * wcol + jnp.where(cs["bdmask"], upd, 0.0)
    return out, pn


def _dn_prep_kernel(prev_ref, main_ref, next_ref, dab_ref, cw_ref, alog_ref, dt_ref,
                    qkv_ref, gates_ref, xbuf, *, tm):
    i = pl.program_id(1)
    n = pl.num_programs(1)
    halo = 8
    pad = (DN_CONV - 1) // 2
    xbuf[0:halo, :] = jnp.where(i == 0, 0.0, prev_ref[0])
    xbuf[halo:halo + tm, :] = main_ref[0]
    xbuf[halo + tm:, :] = jnp.where(i == n - 1, 0.0, next_ref[0])
    ones = _head_ones(LANES)
    for slab in range(3 * DN_WIDTH // LANES):
        cols = slice(slab * LANES, (slab + 1) * LANES)
        y = jnp.zeros((tm, LANES), F32)
        for j in range(DN_CONV):
            y = y + cw_ref[j:j + 1, cols] * xbuf[halo - pad + j:halo - pad + j + tm, cols]
        y = y * _sigmoid(y)
        if slab < 6:
            ss = _mmx(y * y, ones)
            y = y * lax.rsqrt(ss + 1e-6)
        if slab < 3:
            y = y * (HEAD_DIM ** -0.5)
        qkv_ref[0, :, cols] = y
    ab = dab_ref[0]
    lane = _iota((CHUNK, LANES), 1)
    g = -jnp.exp(alog_ref[...]) * _softplus(ab + dt_ref[...])
    beta = _sigmoid(ab)
    tm_f, tm_b = _cumsum_mats()
    for c in range(tm // CHUNK):
        rows = slice(c * CHUNK, (c + 1) * CHUNK)
        gc = jnp.where(lane < DN_HEADS, _xmm(tm_f, g[rows]), _xmm(tm_b, g[rows]))
        gates_ref[0, rows, :] = jnp.where(lane < 2 * DN_HEADS, gc, beta[rows])


def _dn_prep(dqkv3, dab3, conv_w, a_log, dt_bias, tm=512):
    B, S, W = dqkv3.shape
    tm = min(tm, S)
    nb8 = tm // 8
    last8 = S // 8 - 1
    cw = jnp.pad(conv_w, ((0, 8 - DN_CONV), (0, 0)))
    pad_row = lambda a: jnp.pad(a.reshape(1, -1), ((0, 0), (0, LANES - a.size)))
    small = lambda a: pl.BlockSpec(a.shape, lambda b, i: (0,) * a.ndim)
    alog, dt = pad_row(a_log), pad_row(dt_bias)
    return pl.pallas_call(
        functools.partial(_dn_prep_kernel, tm=tm),
        grid=(B, S // tm),
        in_specs=[pl.BlockSpec((1, 8, W), lambda b, i: (b, jnp.maximum(i * nb8 - 1, 0), 0)),
                  pl.BlockSpec((1, tm, W), lambda b, i: (b, i, 0)),
                  pl.BlockSpec((1, 8, W), lambda b, i: (b, jnp.minimum((i + 1) * nb8, last8), 0)),
                  pl.BlockSpec((1, tm, LANES), lambda b, i: (b, i, 0)),
                  small(cw), small(alog), small(dt)],
        out_specs=[pl.BlockSpec((1, tm, W), lambda b, i: (b, i, 0)),
                   pl.BlockSpec((1, tm, LANES), lambda b, i: (b, i, 0))],
        out_shape=[jax.ShapeDtypeStruct((B, S, W), F32),
                   jax.ShapeDtypeStruct((B, S, LANES), F32)],
        scratch_shapes=[pltpu.VMEM((tm + 16, W), F32)],
        compiler_params=_params(("parallel", "parallel")),
        name="deltanet_prep",
    )(dqkv3, dqkv3, dqkv3, dab3, cw, alog, dt)


def _gate_select(base):
    r = _iota((LANES, 2 * LANES), 0)
    c = _iota((LANES, 2 * LANES), 1)
    src = base + ((c >> HEAD_SHIFT) & 1) + jnp.where(c >= LANES, 2 * DN_HEADS, 0)
    return jnp.where(r == src, 1.0, 0.0).astype(BF16)


def _dn_scan_kernel(qf_ref, gf_ref, tf_ref, qb_ref, gb_ref, tb_ref, of_ref, ob_ref, state, *, cpb):
    n = pl.program_id(1)

    @pl.when(n == 0)
    def _():
        state[...] = jnp.zeros_like(state)

    pairs = DN_HEADS // 2
    gens, first, pred, dest = [], [], [], []
    last = {}
    consts = [_chunk_consts(reverse=False), _chunk_consts(reverse=True)]
    sels = [[_gate_select(d * DN_HEADS + 2 * p) for p in range(pairs)] for d in range(2)]
    for step in range(cpb):
        for d, (q_ref, g_ref, t_ref, o_ref) in enumerate(((qf_ref, gf_ref, tf_ref, of_ref),
                                                          (qb_ref, gb_ref, tb_ref, ob_ref))):
            c = step if d == 0 else cpb - 1 - step
            rows = slice(c * CHUNK, (c + 1) * CHUNK)
            gates = g_ref[0, rows, :]
            gc_rows = t_ref[0, c]
            for p in range(pairs):
                q2 = q_ref[0, rows, p * LANES:(p + 1) * LANES]
                k2 = q_ref[0, rows, DN_WIDTH + p * LANES:DN_WIDTH + (p + 1) * LANES]
                v2 = q_ref[0, rows, 2 * DN_WIDTH + p * LANES:2 * DN_WIDTH + (p + 1) * LANES]
                gc_row = gc_rows[d * pairs + p:d * pairs + p + 1, :]
                pred.append(last.get((d, p)))
                first.append(state[d, p] if step == 0 else None)
                last[(d, p)] = len(gens)
                dest.append((o_ref, rows, p))
                gens.append(_dn_chunk(q2, k2, v2, gates, sels[d][p], gc_row, consts[d]))
    results = _interleave(gens, first, pred)
    for (o_ref, rows, p), (o, _) in zip(dest, results):
        o_ref[0, rows, p * LANES:(p + 1) * LANES] = o
    for (d, p), idx in last.items():
        state[d, p] = results[idx][1]


def _dn_scan(qkv3, gates3, cpb=4):
    B, S, W = qkv3.shape
    blk = cpb * CHUNK
    nblk = S // blk
    nchunk = S // CHUNK
    pairs = DN_HEADS // 2
    gct = gates3[..., :2 * DN_HEADS].reshape(B, nchunk, CHUNK, 2, pairs, 2)
    gct = gct.transpose(0, 1, 3, 4, 5, 2).reshape(B, nchunk, 2 * pairs, LANES)
    gct = jnp.pad(gct, ((0, 0), (0, 0), (0, 8 - 2 * pairs), (0, 0)))
    fwd = lambda b, n: (b, n, 0)
    bwd = lambda b, n: (b, nblk - 1 - n, 0)
    fwd4 = lambda b, n: (b, n, 0, 0)
    bwd4 = lambda b, n: (b, nblk - 1 - n, 0, 0)
    return pl.pallas_call(
        functools.partial(_dn_scan_kernel, cpb=cpb),
        grid=(B, nblk),
        in_specs=[pl.BlockSpec((1, blk, W), fwd), pl.BlockSpec((1, blk, LANES), fwd),
                  pl.BlockSpec((1, cpb, 8, LANES), fwd4),
                  pl.BlockSpec((1, blk, W), bwd), pl.BlockSpec((1, blk, LANES), bwd),
                  pl.BlockSpec((1, cpb, 8, LANES), bwd4)],
        out_specs=[pl.BlockSpec((1, blk, DN_WIDTH), fwd), pl.BlockSpec((1, blk, DN_WIDTH), bwd)],
        out_shape=[jax.ShapeDtypeStruct((B, S, DN_WIDTH), F32)] * 2,
        scratch_shapes=[pltpu.VMEM((2, DN_HEADS // 2, LANES, LANES), F32)],
        compiler_params=_params(("parallel", "arbitrary")),
        name="deltanet_scan",
    )(qkv3, gates3, gct, qkv3, gates3, gct)


RW_OUTS = 12


def _rw_prep_kernel(prev_ref, main_ref, next_ref, mu_ref, w0_ref, a0_ref, kk_ref, ka_ref, rk_ref,
                    wup_ref, aup_ref, gup_ref, *refs, tm):
    outs = refs[:RW_OUTS]
    xbuf, pbuf = refs[RW_OUTS:]
    i = pl.program_id(1)
    n = pl.num_programs(1)
    halo = 8
    xbuf[0:halo, :] = jnp.where(i == 0, 0.0, prev_ref[0])
    xbuf[halo:halo + tm, :] = main_ref[0]
    xbuf[halo + tm:, :] = jnp.where(i == n - 1, 0.0, next_ref[0])
    for slab in range(RW_IN // LANES):
        cols = slice(slab * LANES, (slab + 1) * LANES)
        cur = xbuf[halo:halo + tm, cols]
        prv = xbuf[halo - 1:halo - 1 + tm, cols]
        nxt = xbuf[halo + 1:halo + 1 + tm, cols]
        pbuf[:, cols] = cur + mu_ref[0:1, cols] * (prv - cur) + mu_ref[1:2, cols] * (nxt - cur)
    r = pbuf[:, 0:RW_WIDTH]
    k = pbuf[:, RW_WIDTH:2 * RW_WIDTH]
    v = pbuf[:, 2 * RW_WIDTH:3 * RW_WIDTH]
    low = pbuf[:, 3 * RW_WIDTH:3 * RW_WIDTH + LANES]
    gd = pbuf[:, 3 * RW_WIDTH + LANES:]
    ones = _head_ones(RW_WIDTH)
    r_o, v_o, g_o, bg_o = outs[:4]
    g = _mm3(_sigmoid(gd), gup_ref[...])
    kq = k * kk_ref[...]
    kk = kq * lax.rsqrt(_mmx(kq * kq, ones) + 1e-6)
    bonus = _mmx(r * k * rk_ref[...], ones) * v
    r_o[0] = r
    v_o[0] = v
    g_o[0] = g
    bg_o[0] = bonus * g
    wl = jnp.tanh(low)
    cum = _cumsum_mats()
    for d in range(2):
        at_o, gc_o, kd_o, b_o = outs[4 + 4 * d:8 + 4 * d]
        w = -_softplus(-(w0_ref[d:d + 1, :] + _mm3(wl, wup_ref[d]))) - 0.5
        lw = -jnp.exp(w)
        for c in range(tm // CHUNK):
            rows = slice(c * CHUNK, (c + 1) * CHUNK)
            gc = _xmm(cum[d], lw[rows])
            gc_o[0, rows, :] = gc
            at_o[0, rows, :] = -kk[rows] * jnp.exp(gc - lw[rows])
        a = _sigmoid(a0_ref[d:d + 1, :] + _mm3(low, aup_ref[d]))
        kd_o[0] = k * (1.0 + (a - 1.0) * ka_ref[...])
        b_o[0] = kk * a


def _rw_prep(rwp3, mu, w0, w_up, a0, a_up, g_up, k_k, k_a, r_k, tm=256):
    B, S, W = rwp3.shape
    tm = min(tm, S)
    nb8 = tm // 8
    last8 = S // 8 - 1
    zeros = jnp.zeros((2, W_RANK, RW_WIDTH), F32)
    wup = jnp.concatenate([w_up, zeros], axis=1)
    aup = jnp.concatenate([zeros, a_up], axis=1)
    rowv = lambda a: a.reshape(1, RW_WIDTH)
    small = lambda a: pl.BlockSpec(a.shape, lambda b, i: (0,) * a.ndim)
    params = [mu, w0, a0, rowv(k_k), rowv(k_a), rowv(r_k), wup, aup, g_up]
    tile = pl.BlockSpec((1, tm, RW_WIDTH), lambda b, i: (b, i, 0))
    return pl.pallas_call(
        functools.partial(_rw_prep_kernel, tm=tm),
        grid=(B, S // tm),
        in_specs=[pl.BlockSpec((1, 8, W), lambda b, i: (b, jnp.maximum(i * nb8 - 1, 0), 0)),
                  pl.BlockSpec((1, tm, W), lambda b, i: (b, i, 0)),
                  pl.BlockSpec((1, 8, W), lambda b, i: (b, jnp.minimum((i + 1) * nb8, last8), 0))]
                 + [small(a) for a in params],
        out_specs=[tile] * RW_OUTS,
        out_shape=[jax.ShapeDtypeStruct((B, S, RW_WIDTH), F32)] * RW_OUTS,
        scratch_shapes=[pltpu.VMEM((tm + 16, W), F32), pltpu.VMEM((tm, W), F32)],
        compiler_params=_params(("parallel", "parallel")),
        name="rwkv_prep",
    )(rwp3, rwp3, rwp3, *params)


def _rw_scan_kernel(*refs, cpb):
    ins = refs[:12]
    yf_ref, yb_ref, state = refs[12:]
    n = pl.program_id(1)

    @pl.when(n == 0)
    def _():
        state[...] = jnp.zeros_like(state)

    pairs = RW_HEADS // 2
    gens, first, pred, dest = [], [], [], []
    last = {}
    consts = [_chunk_consts(reverse=False), _chunk_consts(reverse=True)]
    for step in range(cpb):
        for d, o_ref in enumerate((yf_ref, yb_ref)):
            cs = consts[d]
            src = ins[6 * d:6 * d + 6]
            c = step if d == 0 else cpb - 1 - step
            rows = slice(c * CHUNK, (c + 1) * CHUNK)
            for p in range(pairs):
                cols = slice(p * LANES, (p + 1) * LANES)
                args = [ref[0, rows, cols] for ref in src]
                pred.append(last.get((d, p)))
                first.append(state[d, p] if step == 0 else None)
                last[(d, p)] = len(gens)
                dest.append((o_ref, rows, cols))
                gens.append(_rw_chunk(*args, cs))
    results = _interleave(gens, first, pred)
    for (o_ref, rows, cols), (out, _) in zip(dest, results):
        o_ref[0, rows, cols] = out
    for (d, p), idx in last.items():
        state[d, p] = results[idx][1]


def _rw_scan(r, v, dirs, cpb=4):
    B, S, W = r.shape
    blk = cpb * CHUNK
    nblk = S // blk
    fwd = pl.BlockSpec((1, blk, W), lambda b, n: (b, n, 0))
    bwd = pl.BlockSpec((1, blk, W), lambda b, n: (b, nblk - 1 - n, 0))
    return pl.pallas_call(
        functools.partial(_rw_scan_kernel, cpb=cpb),
        grid=(B, nblk),
        in_specs=[fwd] * 6 + [bwd] * 6,
        out_specs=[fwd, bwd],
        out_shape=[jax.ShapeDtypeStruct((B, S, W), F32)] * 2,
        scratch_shapes=[pltpu.VMEM((2, RW_HEADS // 2, LANES, LANES), F32)],
        compiler_params=_params(("parallel", "arbitrary")),
        name="rwkv_scan",
    )(r, v, *dirs[0], r, v, *dirs[1])


def _out_proj_kernel(x_ref, ya_ref, of_ref, ob_ref, z_ref, dnw_ref, yf_ref, yb_ref, g_ref, bg_ref,
                     lnw_ref, lnb_ref, wa_ref, wd_ref, wr_ref, gain_ref, xo_ref, h_ref):
    o = of_ref[...] + ob_ref[...]
    ms = _mmx(o * o, _head_ones(DN_WIDTH)) * (1.0 / HEAD_DIM)
    z = z_ref[...]
    ydn = o * lax.rsqrt(ms + NORM_EPS) * dnw_ref[...] * (z * _sigmoid(z))
    y = yf_ref[...] + yb_ref[...]
    ones = _head_ones(RW_WIDTH)
    mean = _mmx(y, ones) * (1.0 / HEAD_DIM)
    yc = y - mean
    var = _mmx(yc * yc, ones) * (1.0 / HEAD_DIM)
    yrw = (yc * lax.rsqrt(var + RW_LN_EPS) * lnw_ref[...] + lnb_ref[...]) * g_ref[...] + bg_ref[...]
    acc = _mm(ya_ref[...], wa_ref[...])
    acc = acc + _mm(ydn.astype(BF16), wd_ref[...])
    acc = acc + _mm(yrw.astype(BF16), wr_ref[...])
    xn = x_ref[...] + acc
    xo_ref[...] = xn
    ms2 = jnp.mean(xn * xn, axis=-1, keepdims=True)
    h_ref[...] = (xn * lax.rsqrt(ms2 + NORM_EPS) * gain_ref[...]).astype(BF16)


def _out_proj(x2, ya, of, ob, z, dnw, yf, yb, g, bg, lnw, lnb, w_out, gain, tm=512):
    T = x2.shape[0]
    tm = min(tm, T)
    row = lambda a: pl.BlockSpec((tm, a.shape[1]), lambda i: (i, 0))
    full = lambda a: pl.BlockSpec(a.shape, lambda i: (0,) * a.ndim)
    wa = w_out[:ATT_WIDTH].astype(BF16)
    wd = w_out[ATT_WIDTH:ATT_WIDTH + DN_WIDTH].astype(BF16)
    wr = w_out[ATT_WIDTH + DN_WIDTH:].astype(BF16)
    args = [x2, ya, of, ob, z, dnw, yf, yb, g, bg, lnw, lnb, wa, wd, wr, gain]
    is_row = [True, True, True, True, True, False, True, True, True, True, False, False,
              False, False, False, False]
    return pl.pallas_call(
        _out_proj_kernel,
        grid=(T // tm,),
        in_specs=[row(a) if r else full(a) for a, r in zip(args, is_row)],
        out_specs=[pl.BlockSpec((tm, D_MODEL), lambda i: (i, 0))] * 2,
        out_shape=[jax.ShapeDtypeStruct((T, D_MODEL), F32), jax.ShapeDtypeStruct((T, D_MODEL), BF16)],
        compiler_params=_params(("parallel",)),
        name="out_proj",
    )(*args)


FFN_HALO = 16


def _ffn_kernel(prev_ref, main_ref, next_ref, x_ref, wg_ref, wv_ref, cg_ref, cv_ref, wd_ref, fin_ref,
                o_ref, hbuf, acc, *, tm, final_norm):
    i = pl.program_id(1)
    j = pl.program_id(2)
    ni = pl.num_programs(1)
    nj = pl.num_programs(2)

    @pl.when(j == 0)
    def _():
        zero = jnp.zeros((FFN_HALO, D_MODEL), BF16)
        hbuf[0:FFN_HALO, :] = jnp.where(i == 0, zero, prev_ref[0])
        hbuf[FFN_HALO:FFN_HALO + tm, :] = main_ref[0]
        hbuf[FFN_HALO + tm:, :] = jnp.where(i == ni - 1, zero, next_ref[0])
        acc[...] = jnp.zeros_like(acc)

    h = hbuf[...]
    rows = tm + 2 * FFN_HALO

    def conv(u, c_ref):
        um = pltpu.roll(u, 1, 0)
        up = pltpu.roll(u, rows - 1, 0)
        y = c_ref[0:1, :] * um + c_ref[1:2, :] * u + c_ref[2:3, :] * up
        return y[FFN_HALO:FFN_HALO + tm]

    gate = conv(_mm(h, wg_ref[...]), cg_ref)
    val = conv(_mm(h, wv_ref[...]), cv_ref)
    act = (gate * _sigmoid(gate) * val).astype(BF16)
    acc[...] += _mm(act, wd_ref[...])

    @pl.when(j == nj - 1)
    def _():
        xn = x_ref[0] + acc[...]
        if final_norm:
            ms = jnp.mean(xn * xn, axis=-1, keepdims=True)
            xn = xn * lax.rsqrt(ms + NORM_EPS) * fin_ref[...]
        o_ref[0] = xn


def _ffn(h3, x3, w_up, conv_w, w_down, fin_gain, final_norm, tm=512, tf=1408):
    B, S, _ = h3.shape
    tm = min(tm, S)
    nbh = tm // FFN_HALO
    lasth = S // FFN_HALO - 1
    nj = D_FF // tf
    wup = w_up.astype(BF16)
    wdn = w_down.astype(BF16)
    cw = jnp.pad(conv_w, ((0, 8 - conv_w.shape[0]), (0, 0)))
    return pl.pallas_call(
        functools.partial(_ffn_kernel, tm=tm, final_norm=final_norm),
        grid=(B, S // tm, nj),
        in_specs=[
            pl.BlockSpec((1, FFN_HALO, D_MODEL), lambda b, i, j: (b, jnp.maximum(i * nbh - 1, 0), 0)),
            pl.BlockSpec((1, tm, D_MODEL), lambda b, i, j: (b, i, 0)),
            pl.BlockSpec((1, FFN_HALO, D_MODEL),
                         lambda b, i, j: (b, jnp.minimum((i + 1) * nbh, lasth), 0)),
            pl.BlockSpec((1, tm, D_MODEL), lambda b, i, j: (b, i, 0)),
            pl.BlockSpec((D_MODEL, tf), lambda b, i, j: (0, j)),
            pl.BlockSpec((D_MODEL, tf), lambda b, i, j: (0, nj + j)),
            pl.BlockSpec((8, tf), lambda b, i, j: (0, j)),
            pl.BlockSpec((8, tf), lambda b, i, j: (0, nj + j)),
            pl.BlockSpec((tf, D_MODEL), lambda b, i, j: (j, 0)),
            pl.BlockSpec((1, D_MODEL), lambda b, i, j: (0, 0)),
        ],
        out_specs=pl.BlockSpec((1, tm, D_MODEL), lambda b, i, j: (b, i, 0)),
        out_shape=jax.ShapeDtypeStruct((B, S, D_MODEL), F32),
        scratch_shapes=[pltpu.VMEM((tm + 2 * FFN_HALO, D_MODEL), BF16),
                        pltpu.VMEM((tm, D_MODEL), F32)],
        compiler_params=_params(("parallel", "parallel", "arbitrary")),
        name="conv_ffn",
    )(h3, h3, h3, x3, wup, wup, cw, cw, wdn, fin_gain)


def _in_weights(w_in_l):
    a = ATT_WIDTH + 2 * ATT_KV_WIDTH
    d = a + 3 * DN_WIDTH
    z = d + DN_WIDTH
    g = z + 4 * DN_HEADS
    w_ab = jnp.pad(w_in_l[:, z:g], ((0, 0), (0, LANES - 4 * DN_HEADS)))
    groups = (w_in_l[:, :a], w_in_l[:, a:d], w_in_l[:, d:z], w_ab, w_in_l[:, g:])
    return [w.astype(BF16) for w in groups]


def kernel(x, positions, norm_mix, w_in, attn_sink, dn_conv, dn_a_log, dn_dt_bias, dn_norm, rw_mu, rw_w0, rw_w_up, rw_a0, rw_a_up, rw_g_up, rw_k_k, rw_k_a, rw_r_k, rw_ln_w, rw_ln_b, w_out, norm_ffn, ffn_w_up, ffn_conv, ffn_w_down, norm_final):
    B, S, D = x.shape
    T = B * S
    depth = w_in.shape[0]
    tables = _rope_tables(positions)
    x2 = x.reshape(T, D)
    for l in range(depth):
        att, dqkv, dz, dab, rwp = _in_proj(x2, norm_mix[l].reshape(1, D), _in_weights(w_in[l]), tables)
        y_att = _attention(att.reshape(B, S, ATT_OUT), attn_sink[l])
        qkv_act, gates = _dn_prep(dqkv.reshape(B, S, -1), dab.reshape(B, S, LANES),
                                  dn_conv[l], dn_a_log[l], dn_dt_bias[l])
        o_f, o_b = _dn_scan(qkv_act, gates)
        rw = _rw_prep(rwp.reshape(B, S, RW_IN), rw_mu[l], rw_w0[l], rw_w_up[l], rw_a0[l], rw_a_up[l],
                      rw_g_up[l], rw_k_k[l], rw_k_a[l], rw_r_k[l])
        r, v, g, bg = rw[:4]
        y_f, y_b = _rw_scan(r, v, (rw[4:8], rw[8:12]))
        flat = lambda a: a.reshape(T, a.shape[-1])
        x2, h2 = _out_proj(
            x2, flat(y_att), flat(o_f), flat(o_b), dz,
            jnp.tile(dn_norm[l], DN_HEADS).reshape(1, DN_WIDTH),
            flat(y_f), flat(y_b), flat(g), flat(bg),
            rw_ln_w[l].reshape(1, RW_WIDTH), rw_ln_b[l].reshape(1, RW_WIDTH),
            w_out[l], norm_ffn[l].reshape(1, D))
        x2 = _ffn(h2.reshape(B, S, D), x2.reshape(B, S, D), ffn_w_up[l], ffn_conv[l], ffn_w_down[l],
                  norm_final.reshape(1, D), final_norm=(l == depth - 1)).reshape(T, D)
    return x2.reshape(B, S, D)
```

```python
import functools

import jax
import jax.numpy as jnp
from jax import lax
from jax.experimental import pallas as pl
from jax.experimental.pallas import tpu as pltpu

F32 = jnp.float32
BF16 = jnp.bfloat16

D_MODEL = 1024
HEAD_DIM = 64
HEAD_SHIFT = 6
ATT_HEADS = 6
ATT_WIDTH = 384
ATT_KV_WIDTH = 128
WINDOW = 128
ROPE_DIM = 16
ROPE_THETA = 500000.0
DN_HEADS = 6
DN_WIDTH = 384
DN_CONV = 5
RW_HEADS = 4
RW_WIDTH = 256
W_RANK = 64
A_RANK = 64
G_RANK = 128
RW_IN = 1024
D_FF = 2816
NORM_EPS = 1e-6
RW_LN_EPS = 64e-5
LANES = 128
CHUNK = 64
NEG_BIG = -1e30

NN = (((1,), (0,)), ((), ()))
NT = (((1,), (1,)), ((), ()))
TN = (((0,), (0,)), ((), ()))

VMEM_LIMIT = 56 * 1024 * 1024


def _mm(a, b, dims=NN):
    return lax.dot_general(a, b, dims, preferred_element_type=F32)


def _split2(x):
    hi = x.astype(BF16)
    lo = (x - hi.astype(F32)).astype(BF16)
    return hi, lo


def _split3(x):
    hi = x.astype(BF16)
    r = x - hi.astype(F32)
    mid = r.astype(BF16)
    lo = (r - mid.astype(F32)).astype(BF16)
    return hi, mid, lo


def _mm1(a, b, dims=NN):
    return _mm(a.astype(BF16), b.astype(BF16), dims)


_mm_score = _mm1
_mm_neu_y = _mm1
_mm_neu_sq = _mm1
_mm_intra = _mm1
_mm_state = _mm1


def _mmx(a, e, dims=NN):
    h, l = _split2(a)
    return _mm(h, e, dims) + _mm(l, e, dims)


def _xmm(e, a, dims=NN):
    h, m, l = _split3(a)
    return _mm(e, h, dims) + (_mm(e, m, dims) + _mm(e, l, dims))


def _iota(shape, dim):
    return lax.broadcasted_iota(jnp.int32, shape, dim)


def _sigmoid(x):
    return 1.0 / (1.0 + jnp.exp(-x))


def _softplus(x):
    return jnp.maximum(x, 0.0) + jnp.log1p(jnp.exp(-jnp.abs(x)))


def _head_ones(width):
    r = _iota((width, width), 0) >> HEAD_SHIFT
    c = _iota((width, width), 1) >> HEAD_SHIFT
    return jnp.where(r == c, 1.0, 0.0).astype(BF16)


def _params(sem, vmem=VMEM_LIMIT):
    return pltpu.CompilerParams(dimension_semantics=sem, vmem_limit_bytes=vmem)


def _rope_table_kernel(pos_ref, freq_ref, c_ref, s1_ref, s2_ref):
    ang = pos_ref[...].astype(F32) * freq_ref[...]
    c = jnp.cos(ang)
    s = jnp.sin(ang)
    j = _iota(ang.shape, 1) & (HEAD_DIM - 1)
    half = ROPE_DIM // 2
    c_ref[...] = c
    s1_ref[...] = jnp.where(j < half, -s, 0.0)
    s2_ref[...] = jnp.where((j >= half) & (j < ROPE_DIM), s, 0.0)


def _rope_tables(positions):
    T = positions.size
    half = ROPE_DIM // 2
    inv_freq = ROPE_THETA ** (-jnp.arange(half, dtype=F32) / half)
    lane = jnp.arange(LANES)
    freq = jnp.where((lane % HEAD_DIM) < ROPE_DIM, inv_freq[lane % half], 0.0).astype(F32)[None, :]
    pos_b = jnp.broadcast_to(positions.reshape(T, 1), (T, LANES))
    tm = min(T, 1024)
    spec = pl.BlockSpec((tm, LANES), lambda i: (i, 0))
    return pl.pallas_call(
        _rope_table_kernel,
        grid=(T // tm,),
        in_specs=[spec, pl.BlockSpec((1, LANES), lambda i: (0, 0))],
        out_specs=[spec, spec, spec],
        out_shape=[jax.ShapeDtypeStruct((T, LANES), F32)] * 3,
        compiler_params=_params(("parallel",)),
        name="rope_tables",
    )(pos_b, freq)


ATT_OUT = 7 * LANES


def _in_proj_kernel(x_ref, gain_ref, watt_ref, wdqkv_ref, wdz_ref, wab_ref, wrw_ref,
                    c_ref, s1_ref, s2_ref,
                    att_ref, dqkv_ref, dz_ref, dab_ref, rwp_ref):
    x = x_ref[...]
    ms = jnp.mean(x * x, axis=-1, keepdims=True)
    h = (x * lax.rsqrt(ms + NORM_EPS) * gain_ref[...]).astype(BF16)
    att = _mm(h, watt_ref[...])
    c, s1, s2 = c_ref[...], s1_ref[...], s2_ref[...]
    half = ROPE_DIM // 2
    for slab in range(4):
        t = att[:, slab * LANES:(slab + 1) * LANES]
        t = t * c + pltpu.roll(t, LANES - half, 1) * s1 + pltpu.roll(t, half, 1) * s2
        if slab < 3:
            att_ref[:, slab * LANES:(slab + 1) * LANES] = (t * (HEAD_DIM ** -0.5)).astype(BF16)
        else:
            att_ref[:, 3 * LANES:4 * LANES] = t.astype(BF16)
            att_ref[:, 5 * LANES:6 * LANES] = pltpu.roll(t, HEAD_DIM, 1).astype(BF16)
    v = att[:, 4 * LANES:5 * LANES]
    att_ref[:, 4 * LANES:5 * LANES] = v.astype(BF16)
    att_ref[:, 6 * LANES:7 * LANES] = pltpu.roll(v, HEAD_DIM, 1).astype(BF16)
    dqkv_ref[...] = _mm(h, wdqkv_ref[...])
    dz_ref[...] = _mm(h, wdz_ref[...])
    dab_ref[...] = _mm(h, wab_ref[...])
    rwp_ref[...] = _mm(h, wrw_ref[...])


def _in_proj(x2, gain, w, tables, tm=512):
    T = x2.shape[0]
    tm = min(tm, T)
    row = lambda wdt: pl.BlockSpec((tm, wdt), lambda i: (i, 0))
    full = lambda a: pl.BlockSpec(a.shape, lambda i: (0,) * a.ndim)
    widths = (ATT_OUT, 3 * DN_WIDTH, DN_WIDTH, LANES, RW_IN)
    dts = (BF16, F32, F32, F32, F32)
    return pl.pallas_call(
        _in_proj_kernel,
        grid=(T // tm,),
        in_specs=[row(D_MODEL), full(gain)] + [full(a) for a in w] + [row(LANES)] * 3,
        out_specs=[row(wd) for wd in widths],
        out_shape=[jax.ShapeDtypeStruct((T, wd), dt) for wd, dt in zip(widths, dts)],
        compiler_params=_params(("parallel",)),
        name="in_proj",
    )(x2, gain, *w, *tables)


def _attn_kernel(sink_ref, q_ref, *refs, tq, seq):
    kv_refs = refs[:12]
    o_ref = refs[12]
    win = refs[13:17]
    i = pl.program_id(1)
    for a in range(4):
        prev, main, nxt = kv_refs[3 * a:3 * a + 3]
        win[a][0:WINDOW, :] = prev[0]
        win[a][WINDOW:WINDOW + tq, :] = main[0]
        win[a][WINDOW + tq:, :] = nxt[0]
    kA, vA, kB, vB = win
    blk = WINDOW
    lane_hi = _iota((3 * blk, LANES), 1) >= HEAD_DIM
    qlane_hi = _iota((blk, LANES), 1) >= HEAD_DIM
    qi = _iota((blk, 3 * blk), 0)
    kk = _iota((blk, 3 * blk), 1)
    rel = kk - blk - qi
    band = (rel <= WINDOW) & (rel >= -WINDOW)
    for s in range(tq // blk):
        r0 = s * blk
        abs_k = i * tq + r0 + kk - blk
        ok = band & (abs_k >= 0) & (abs_k < seq)
        kwin = {0: kA[r0:r0 + 3 * blk, :], 1: kB[r0:r0 + 3 * blk, :]}
        vwin = {0: vA[r0:r0 + 3 * blk, :], 1: vB[r0:r0 + 3 * blk, :]}
        src = [0 if (h // 3) == (h % 2) else 1 for h in range(ATT_HEADS)]
        scores = []
        for h in range(ATT_HEADS):
            qp = q_ref[0, r0:r0 + blk, (h // 2) * LANES:(h // 2 + 1) * LANES]
            is_half = qlane_hi if h % 2 == 1 else jnp.logical_not(qlane_hi)
            qm = jnp.where(is_half, qp, jnp.zeros_like(qp))
            scores.append(_mm(qm, kwin[src[h]], NT))
        probs = []
        for h in range(ATT_HEADS):
            sc = jnp.where(ok, scores[h], NEG_BIG)
            sk = sink_ref[h]
            m = jnp.maximum(jnp.max(sc, axis=-1, keepdims=True), sk)
            pexp = jnp.exp(sc - m)
            den = jnp.sum(pexp, axis=-1, keepdims=True) + jnp.exp(sk - m)
            probs.append((pexp / den).astype(BF16))
        for p in range(ATT_HEADS // 2):
            acc = jnp.zeros((blk, LANES), F32)
            for half in range(2):
                h = 2 * p + half
                v_half = lane_hi if half == 1 else jnp.logical_not(lane_hi)
                vm = jnp.where(v_half, vwin[src[h]], jnp.zeros_like(vwin[src[h]]))
                acc = acc + _mm(probs[h], vm)
            o_ref[0, r0:r0 + blk, p * LANES:(p + 1) * LANES] = acc.astype(o_ref.dtype)


def _attention(att3, sink, tq=512):
    B, S, _ = att3.shape
    tq = min(tq, S)
    nb = tq // WINDOW
    last = S // WINDOW - 1
    specs = [pl.BlockSpec(memory_space=pltpu.SMEM),
             pl.BlockSpec((1, tq, ATT_WIDTH), lambda b, i: (b, i, 0))]
    for col in (3, 4, 5, 6):
        specs.append(pl.BlockSpec((1, WINDOW, LANES),
                                  lambda b, i, col=col: (b, jnp.maximum(i * nb - 1, 0), col)))
        specs.append(pl.BlockSpec((1, tq, LANES), lambda b, i, col=col: (b, i, col)))
        specs.append(pl.BlockSpec((1, WINDOW, LANES),
                                  lambda b, i, col=col: (b, jnp.minimum((i + 1) * nb, last), col)))
    return pl.pallas_call(
        functools.partial(_attn_kernel, tq=tq, seq=S),
        grid=(B, S // tq),
        in_specs=specs,
        out_specs=pl.BlockSpec((1, tq, ATT_WIDTH), lambda b, i: (b, i, 0)),
        out_shape=jax.ShapeDtypeStruct((B, S, ATT_WIDTH), BF16),
        scratch_shapes=[pltpu.VMEM((tq + 2 * WINDOW, LANES), BF16)] * 4,
        compiler_params=_params(("parallel", "parallel")),
        name="window_attention",
    )(sink, att3, *([att3] * 12))


def _cumsum_mats():
    rc = _iota((CHUNK, CHUNK), 0)
    cc = _iota((CHUNK, CHUNK), 1)
    return (jnp.where(cc <= rc, 1.0, 0.0).astype(BF16), jnp.where(cc >= rc, 1.0, 0.0).astype(BF16))


def _chunk_consts(reverse):
    C = CHUNK
    ri = _iota((C, LANES), 0)
    li = _iota((C, LANES), 1)
    j = li & (HEAD_DIM - 1)
    r2 = _iota((LANES, LANES), 0)
    c2 = _iota((LANES, LANES), 1)
    if reverse:
        incl, strict = j >= ri, j > ri
    else:
        incl, strict = j <= ri, j < ri
    return dict(
        incl=incl, strict=strict, hi=li >= HEAD_DIM,
        bdmask=(r2 >= HEAD_DIM) == (c2 >= HEAD_DIM),
        tot_row=0 if reverse else C - 1,
    )


def _bd(y):
    hi = (_iota(y.shape, 1) & (LANES - 1)) >= HEAD_DIM
    zero = jnp.zeros_like(y)
    return jnp.concatenate([jnp.where(hi, zero, y), jnp.where(hi, y, zero)], axis=0)


def _apply(x2, y, mm):
    return mm(x2, _bd(y))


def _xsel(a, e):
    h, m, l = _split3(a)
    return _mm(h, e) + (_mm(m, e) + _mm(l, e))


def _gate_select(base):
    r = _iota((LANES, 2 * LANES), 0)
    c = _iota((LANES, 2 * LANES), 1)
    src = base + ((c >> HEAD_SHIFT) & 1) + jnp.where(c >= LANES, 2 * DN_HEADS, 0)
    return jnp.where(r == src, 1.0, 0.0).astype(BF16)


_NEED_STATE = object()


def _neumann(n2, y):
    eye = (_iota(n2.shape, 1) & (HEAD_DIM - 1)) == _iota(n2.shape, 0)
    t = n2 + jnp.where(eye, 1.0, 0.0)
    pw = _apply(n2, n2, _mm_neu_sq)
    yield
    levels = CHUNK.bit_length() - 2
    for _ in range(levels):
        both = _apply(pw, jnp.concatenate([t, pw], axis=1), _mm_neu_sq)
        t = t + both[:, :LANES]
        pw = both[:, LANES:]
        yield
    return _apply(t, y, _mm_neu_y)


def _interleave(gens, first_state, pred):
    n = len(gens)
    results = [None] * n
    waiting = [False] * n
    done = [False] * n
    while not all(done):
        for i in range(n):
            if done[i]:
                continue
            try:
                if waiting[i]:
                    if pred[i] is None:
                        state = first_state[i]
                    elif done[pred[i]]:
                        state = results[pred[i]][1]
                    else:
                        continue
                    waiting[i] = False
                    token = gens[i].send(state)
                else:
                    token = next(gens[i])
                if token is _NEED_STATE:
                    waiting[i] = True
            except StopIteration as stop:
                results[i] = stop.value
                done[i] = True
    return results


def _dn_chunk(q2, k2, v2, gates, base, gc_row, cs):
    C = CHUNK
    gx = _xsel(gates, _gate_select(base))
    yield
    gcb, bb = gx[:, :LANES], gx[:, LANES:]
    kb = k2 * bb
    vb = v2 * bb
    s2 = _mm_score(jnp.concatenate([kb, q2], axis=0), _bd(k2), NT)
    row = cs["tot_row"]
    gtot = jnp.broadcast_to(gcb[row:row + 1, :], (C, LANES))
    eg = jnp.exp(gcb)
    kbe = kb * eg
    qg = q2 * eg
    kg = k2 * jnp.exp(gtot - gcb)
    rj = jnp.broadcast_to(gc_row, (C, LANES))
    yield
    dec = jnp.exp(jnp.where(cs["incl"], gcb - rj, NEG_BIG))
    l2 = jnp.where(cs["strict"], s2[:C] * dec, 0.0)
    qk = s2[C:] * dec
    y = yield from _neumann(-l2, jnp.concatenate([vb, kbe], axis=1))
    value, kcum = y[:, :LANES], y[:, LANES:]
    return value, kcum, qg, kg, qk, gtot


def _dn_state(load, cs):
    C = CHUNK
    P = yield _NEED_STATE
    sp = _mm_state(jnp.concatenate([load(1), load(2)], axis=0), P)
    yield
    vnew = load(0) - sp[:C]
    o = sp[C:] + _apply(load(4), vnew, _mm_state)
    upd = _mm_state(load(3), vnew, TN)
    yield
    gtot = load(5)
    glast = jnp.exp(jnp.concatenate([gtot, gtot], axis=0))
    pn = P * glast + jnp.where(cs["bdmask"], upd, 0.0)
    return o, pn


def _rw_chunk(r2, v2, at, gcb, kd2, b2, cs):
    C = CHUNK
    row = cs["tot_row"]
    gtot = jnp.broadcast_to(gcb[row:row + 1, :], (C, LANES))
    eneg = jnp.exp(-gcb)
    bt = b2 * eneg
    kt = kd2 * eneg
    rt = r2 * jnp.exp(gcb)
    er = jnp.exp(gtot - gcb)
    bh = b2 * er
    kh = kd2 * er
    bk = jnp.concatenate([_bd(bt), _bd(kt)], axis=0)
    sa = _mm_score(at, bk, NT)
    sr = _mm_score(rt, bk, NT)
    yield
    aab = jnp.where(cs["strict"], sa[:, :LANES], 0.0)
    aak = jnp.where(cs["strict"], sa[:, LANES:], 0.0)
    arb = jnp.where(cs["incl"], sr[:, :LANES], 0.0)
    ark = jnp.where(cs["incl"], sr[:, LANES:], 0.0)
    av = _apply(aak, v2, _mm_intra)
    arkv = _apply(ark, v2, _mm_intra)
    yield
    y = yield from _neumann(aab, jnp.concatenate([av, at], axis=1))
    vp, ap = y[:, :LANES], y[:, LANES:]
    return vp, ap, rt, arb, arkv, bh, kh, v2, gtot


def _rw_state(load, cs):
    C = CHUNK
    P = yield _NEED_STATE
    sp = _mm_state(jnp.concatenate([load(1), load(2)], axis=0), P)
    yield
    u = sp[:C] + load(0)
    v2 = load(7)
    out = sp[C:] + _apply(load(3), u, _mm_state) + load(4)
    upd = _mm_state(jnp.concatenate([load(5), load(6)], axis=0), jnp.concatenate([u, v2], axis=0), TN)
    yield
    gtot = load(8)
    wtot = jnp.exp(jnp.concatenate([gtot, gtot], axis=0))
    pn = P * wtot.T + jnp.where(cs["bdmask"], upd, 0.0)
    return out, pn


def _dn_prep_kernel(prev_ref, main_ref, next_ref, dab_ref, cw_ref, alog_ref, dt_ref,
                    qkv_ref, gates_ref, xbuf, *, tm):
    i = pl.program_id(1)
    n = pl.num_programs(1)
    halo = 8
    pad = (DN_CONV - 1) // 2
    xbuf[0:halo, :] = jnp.where(i == 0, 0.0, prev_ref[0])
    xbuf[halo:halo + tm, :] = main_ref[0]
    xbuf[halo + tm:, :] = jnp.where(i == n - 1, 0.0, next_ref[0])
    ones = _head_ones(LANES)
    for slab in range(3 * DN_WIDTH // LANES):
        cols = slice(slab * LANES, (slab + 1) * LANES)
        y = jnp.zeros((tm, LANES), F32)
        for j in range(DN_CONV):
            y = y + cw_ref[j:j + 1, cols] * xbuf[halo - pad + j:halo - pad + j + tm, cols]
        y = y * _sigmoid(y)
        if slab < 6:
            ss = _mmx(y * y, ones)
            y = y * lax.rsqrt(ss + 1e-6)
        if slab < 3:
            y = y * (HEAD_DIM ** -0.5)
        qkv_ref[0, :, cols] = y
    ab = dab_ref[0]
    lane = _iota((CHUNK, LANES), 1)
    g = -jnp.exp(alog_ref[...]) * _softplus(ab + dt_ref[...])
    beta = _sigmoid(ab)
    tm_f, tm_b = _cumsum_mats()
    for c in range(tm // CHUNK):
        rows = slice(c * CHUNK, (c + 1) * CHUNK)
        gc = jnp.where(lane < DN_HEADS, _xmm(tm_f, g[rows]), _xmm(tm_b, g[rows]))
        gates_ref[0, rows, :] = jnp.where(lane < 2 * DN_HEADS, gc, beta[rows])


def _dn_prep(dqkv3, dab3, conv_w, a_log, dt_bias, tm=512):
    B, S, W = dqkv3.shape
    tm = min(tm, S)
    nb8 = tm // 8
    last8 = S // 8 - 1
    cw = jnp.pad(conv_w, ((0, 8 - DN_CONV), (0, 0)))
    pad_row = lambda a: jnp.pad(a.reshape(1, -1), ((0, 0), (0, LANES - a.size)))
    small = lambda a: pl.BlockSpec(a.shape, lambda b, i: (0,) * a.ndim)
    alog, dt = pad_row(a_log), pad_row(dt_bias)
    return pl.pallas_call(
        functools.partial(_dn_prep_kernel, tm=tm),
        grid=(B, S // tm),
        in_specs=[pl.BlockSpec((1, 8, W), lambda b, i: (b, jnp.maximum(i * nb8 - 1, 0), 0)),
                  pl.BlockSpec((1, tm, W), lambda b, i: (b, i, 0)),
                  pl.BlockSpec((1, 8, W), lambda b, i: (b, jnp.minimum((i + 1) * nb8, last8), 0)),
                  pl.BlockSpec((1, tm, LANES), lambda b, i: (b, i, 0)),
                  small(cw), small(alog), small(dt)],
        out_specs=[pl.BlockSpec((1, tm, W), lambda b, i: (b, i, 0)),
                   pl.BlockSpec((1, tm, LANES), lambda b, i: (b, i, 0))],
        out_shape=[jax.ShapeDtypeStruct((B, S, W), F32),
                   jax.ShapeDtypeStruct((B, S, LANES), F32)],
        scratch_shapes=[pltpu.VMEM((tm + 16, W), F32)],
        compiler_params=_params(("parallel", "parallel")),
        name="deltanet_prep",
    )(dqkv3, dqkv3, dqkv3, dab3, cw, alog, dt)


def _scan_step(state, insts, second):
    def whole(gen, cs, write):
        mid = yield from gen
        o, pn = yield from second(lambda a: mid[a], cs)
        write(o)
        return None, pn

    gens, pred, first = [], [], []
    last = {}
    for i, (chain, gen, cs, write) in enumerate(insts):
        gens.append(whole(gen, cs, write))
        pred.append(last.get(chain))
        first.append(None if chain in last else state[chain])
        last[chain] = i
    results = _interleave(gens, first, pred)
    for chain, i in last.items():
        state[chain] = results[i][1]


def _scan_reset(state):
    @pl.when(pl.program_id(1) == 0)
    def _():
        state[...] = jnp.zeros_like(state)


def _dn_scan_kernel(qf_ref, gf_ref, tf_ref, qb_ref, gb_ref, tb_ref, of_ref, ob_ref, state, *, cpb):
    _scan_reset(state)
    pairs = DN_HEADS // 2
    consts = [_chunk_consts(reverse=False), _chunk_consts(reverse=True)]
    insts = []
    for step in range(cpb):
        for d, (q_ref, g_ref, t_ref, o_ref) in enumerate(((qf_ref, gf_ref, tf_ref, of_ref),
                                                          (qb_ref, gb_ref, tb_ref, ob_ref))):
            c = step if d == 0 else cpb - 1 - step
            rows = slice(c * CHUNK, (c + 1) * CHUNK)
            gates = g_ref[0, rows, :]
            gc_rows = t_ref[0, c]
            for p in range(pairs):
                cols = slice(p * LANES, (p + 1) * LANES)
                q2 = q_ref[0, rows, p * LANES:(p + 1) * LANES]
                k2 = q_ref[0, rows, DN_WIDTH + p * LANES:DN_WIDTH + (p + 1) * LANES]
                v2 = q_ref[0, rows, 2 * DN_WIDTH + p * LANES:2 * DN_WIDTH + (p + 1) * LANES]
                gc_row = gc_rows[d * pairs + p:d * pairs + p + 1, :]
                gen = _dn_chunk(q2, k2, v2, gates, d * DN_HEADS + 2 * p, gc_row, consts[d])

                def write(o, o_ref=o_ref, rows=rows, cols=cols):
                    o_ref[0, rows, cols] = o

                insts.append(((d, p), gen, consts[d], write))
    _scan_step(state, insts, _dn_state)


def _scan_maps(nblk, ndim):
    tail = (0,) * (ndim - 2)
    return (lambda b, n: (b, n) + tail, lambda b, n: (b, nblk - 1 - n) + tail)


def _dn_scan(qkv3, gates3, cpb=8):
    B, S, W = qkv3.shape
    blk = cpb * CHUNK
    nblk = S // blk
    nchunk = S // CHUNK
    pairs = DN_HEADS // 2
    gct = gates3[..., :2 * DN_HEADS].reshape(B, nchunk, CHUNK, 2, pairs, 2)
    gct = gct.transpose(0, 1, 3, 4, 5, 2).reshape(B, nchunk, 2 * pairs, LANES)
    gct = jnp.pad(gct, ((0, 0), (0, 0), (0, 8 - 2 * pairs), (0, 0)))
    fwd, bwd = _scan_maps(nblk, 3)
    fwd4, bwd4 = _scan_maps(nblk, 4)
    return pl.pallas_call(
        functools.partial(_dn_scan_kernel, cpb=cpb),
        grid=(B, nblk),
        in_specs=[pl.BlockSpec((1, blk, W), fwd), pl.BlockSpec((1, blk, LANES), fwd),
                  pl.BlockSpec((1, cpb, 8, LANES), fwd4),
                  pl.BlockSpec((1, blk, W), bwd), pl.BlockSpec((1, blk, LANES), bwd),
                  pl.BlockSpec((1, cpb, 8, LANES), bwd4)],
        out_specs=[pl.BlockSpec((1, blk, DN_WIDTH), fwd), pl.BlockSpec((1, blk, DN_WIDTH), bwd)],
        out_shape=[jax.ShapeDtypeStruct((B, S, DN_WIDTH), F32)] * 2,
        scratch_shapes=[pltpu.VMEM((2, DN_HEADS // 2, LANES, LANES), F32)],
        compiler_params=_params(("parallel", "arbitrary")),
        name="deltanet_scan",
    )(qkv3, gates3, gct, qkv3, gates3, gct)


RW_OUTS = 12


def _rw_prep_kernel(prev_ref, main_ref, next_ref, mu_ref, w0_ref, a0_ref, kk_ref, ka_ref, rk_ref,
                    wup_ref, aup_ref, gup_ref, *refs, tm):
    outs = refs[:RW_OUTS]
    xbuf, pbuf = refs[RW_OUTS:]
    i = pl.program_id(1)
    n = pl.num_programs(1)
    halo = 8
    xbuf[0:halo, :] = jnp.where(i == 0, 0.0, prev_ref[0])
    xbuf[halo:halo + tm, :] = main_ref[0]
    xbuf[halo + tm:, :] = jnp.where(i == n - 1, 0.0, next_ref[0])
    for slab in range(RW_IN // LANES):
        cols = slice(slab * LANES, (slab + 1) * LANES)
        cur = xbuf[halo:halo + tm, cols]
        prv = xbuf[halo - 1:halo - 1 + tm, cols]
        nxt = xbuf[halo + 1:halo + 1 + tm, cols]
        pbuf[:, cols] = cur + mu_ref[0:1, cols] * (prv - cur) + mu_ref[1:2, cols] * (nxt - cur)
    r = pbuf[:, 0:RW_WIDTH]
    k = pbuf[:, RW_WIDTH:2 * RW_WIDTH]
    v = pbuf[:, 2 * RW_WIDTH:3 * RW_WIDTH]
    low = pbuf[:, 3 * RW_WIDTH:3 * RW_WIDTH + LANES]
    gd = pbuf[:, 3 * RW_WIDTH + LANES:]
    ones = _head_ones(RW_WIDTH)
    r_o, v_o, g_o, bg_o = outs[:4]
    g = _mm1(_sigmoid(gd), gup_ref[...])
    kq = k * kk_ref[...]
    kk = kq * lax.rsqrt(_mmx(kq * kq, ones) + 1e-6)
    bonus = _mmx(r * k * rk_ref[...], ones) * v
    r_o[0] = r
    v_o[0] = v
    g_o[0] = g
    bg_o[0] = bonus * g
    wl = jnp.tanh(low)
    cum = _cumsum_mats()
    for d in range(2):
        at_o, gc_o, kd_o, b_o = outs[4 + 4 * d:8 + 4 * d]
        w = -_softplus(-(w0_ref[d:d + 1, :] + _mm1(wl, wup_ref[d]))) - 0.5
        lw = -jnp.exp(w)
        for c in range(tm // CHUNK):
            rows = slice(c * CHUNK, (c + 1) * CHUNK)
            gc = _xmm(cum[d], lw[rows])
            gc_o[0, rows, :] = gc
            at_o[0, rows, :] = -kk[rows] * jnp.exp(gc - lw[rows])
        a = _sigmoid(a0_ref[d:d + 1, :] + _mm1(low, aup_ref[d]))
        kd_o[0] = k * (1.0 + (a - 1.0) * ka_ref[...])
        b_o[0] = kk * a


def _rw_prep(rwp3, mu, w0, w_up, a0, a_up, g_up, k_k, k_a, r_k, tm=256):
    B, S, W = rwp3.shape
    tm = min(tm, S)
    nb8 = tm // 8
    last8 = S // 8 - 1
    zeros = jnp.zeros((2, W_RANK, RW_WIDTH), F32)
    wup = jnp.concatenate([w_up, zeros], axis=1)
    aup = jnp.concatenate([zeros, a_up], axis=1)
    rowv = lambda a: a.reshape(1, RW_WIDTH)
    small = lambda a: pl.BlockSpec(a.shape, lambda b, i: (0,) * a.ndim)
    params = [mu, w0, a0, rowv(k_k), rowv(k_a), rowv(r_k), wup, aup, g_up]
    tile = pl.BlockSpec((1, tm, RW_WIDTH), lambda b, i: (b, i, 0))
    return pl.pallas_call(
        functools.partial(_rw_prep_kernel, tm=tm),
        grid=(B, S // tm),
        in_specs=[pl.BlockSpec((1, 8, W), lambda b, i: (b, jnp.maximum(i * nb8 - 1, 0), 0)),
                  pl.BlockSpec((1, tm, W), lambda b, i: (b, i, 0)),
                  pl.BlockSpec((1, 8, W), lambda b, i: (b, jnp.minimum((i + 1) * nb8, last8), 0))]
                 + [small(a) for a in params],
        out_specs=[tile] * RW_OUTS,
        out_shape=[jax.ShapeDtypeStruct((B, S, RW_WIDTH), F32)] * RW_OUTS,
        scratch_shapes=[pltpu.VMEM((tm + 16, W), F32), pltpu.VMEM((tm, W), F32)],
        compiler_params=_params(("parallel", "parallel")),
        name="rwkv_prep",
    )(rwp3, rwp3, rwp3, *params)


def _rw_scan_kernel(*refs, cpb):
    ins = refs[:12]
    yf_ref, yb_ref, state = refs[12:]
    _scan_reset(state)
    pairs = RW_HEADS // 2
    consts = [_chunk_consts(reverse=False), _chunk_consts(reverse=True)]
    insts = []
    for step in range(cpb):
        for d, o_ref in enumerate((yf_ref, yb_ref)):
            src = ins[6 * d:6 * d + 6]
            c = step if d == 0 else cpb - 1 - step
            rows = slice(c * CHUNK, (c + 1) * CHUNK)
            for p in range(pairs):
                cols = slice(p * LANES, (p + 1) * LANES)
                gen = _rw_chunk(*[ref[0, rows, cols] for ref in src], consts[d])

                def write(o, o_ref=o_ref, rows=rows, cols=cols):
                    o_ref[0, rows, cols] = o

                insts.append(((d, p), gen, consts[d], write))
    _scan_step(state, insts, _rw_state)


def _rw_scan(r, v, dirs, cpb=8):
    B, S, W = r.shape
    blk = cpb * CHUNK
    nblk = S // blk
    fwd, bwd = [pl.BlockSpec((1, blk, W), m) for m in _scan_maps(nblk, 3)]
    return pl.pallas_call(
        functools.partial(_rw_scan_kernel, cpb=cpb),
        grid=(B, nblk),
        in_specs=[fwd] * 6 + [bwd] * 6,
        out_specs=[fwd, bwd],
        out_shape=[jax.ShapeDtypeStruct((B, S, W), F32)] * 2,
        scratch_shapes=[pltpu.VMEM((2, RW_HEADS // 2, LANES, LANES), F32)],
        compiler_params=_params(("parallel", "arbitrary")),
        name="rwkv_scan",
    )(r, v, *dirs[0], r, v, *dirs[1])


def _out_proj_kernel(x_ref, ya_ref, of_ref, ob_ref, z_ref, dnw_ref, yf_ref, yb_ref, g_ref, bg_ref,
                     lnw_ref, lnb_ref, wa_ref, wd_ref, wr_ref, gain_ref, xo_ref, h_ref):
    o = of_ref[...] + ob_ref[...]
    ms = _mmx(o * o, _head_ones(DN_WIDTH)) * (1.0 / HEAD_DIM)
    z = z_ref[...]
    ydn = o * lax.rsqrt(ms + NORM_EPS) * dnw_ref[...] * (z * _sigmoid(z))
    y = yf_ref[...] + yb_ref[...]
    ones = _head_ones(RW_WIDTH)
    mean = _mmx(y, ones) * (1.0 / HEAD_DIM)
    yc = y - mean
    var = _mmx(yc * yc, ones) * (1.0 / HEAD_DIM)
    yrw = (yc * lax.rsqrt(var + RW_LN_EPS) * lnw_ref[...] + lnb_ref[...]) * g_ref[...] + bg_ref[...]
    acc = _mm(ya_ref[...], wa_ref[...])
    acc = acc + _mm(ydn.astype(BF16), wd_ref[...])
    acc = acc + _mm(yrw.astype(BF16), wr_ref[...])
    xn = x_ref[...] + acc
    xo_ref[...] = xn
    ms2 = jnp.mean(xn * xn, axis=-1, keepdims=True)
    h_ref[...] = (xn * lax.rsqrt(ms2 + NORM_EPS) * gain_ref[...]).astype(BF16)


def _out_proj(x2, ya, of, ob, z, dnw, yf, yb, g, bg, lnw, lnb, w_out, gain, tm=512):
    T = x2.shape[0]
    tm = min(tm, T)
    row = lambda a: pl.BlockSpec((tm, a.shape[1]), lambda i: (i, 0))
    full = lambda a: pl.BlockSpec(a.shape, lambda i: (0,) * a.ndim)
    wa = w_out[:ATT_WIDTH].astype(BF16)
    wd = w_out[ATT_WIDTH:ATT_WIDTH + DN_WIDTH].astype(BF16)
    wr = w_out[ATT_WIDTH + DN_WIDTH:].astype(BF16)
    args = [x2, ya, of, ob, z, dnw, yf, yb, g, bg, lnw, lnb, wa, wd, wr, gain]
    is_row = [True, True, True, True, True, False, True, True, True, True, False, False,
              False, False, False, False]
    return pl.pallas_call(
        _out_proj_kernel,
        grid=(T // tm,),
        in_specs=[row(a) if r else full(a) for a, r in zip(args, is_row)],
        out_specs=[pl.BlockSpec((tm, D_MODEL), lambda i: (i, 0))] * 2,
        out_shape=[jax.ShapeDtypeStruct((T, D_MODEL), F32), jax.ShapeDtypeStruct((T, D_MODEL), BF16)],
        compiler_params=_params(("parallel",)),
        name="out_proj",
    )(*args)


FFN_HALO = 16


def _ffn_kernel(prev_ref, main_ref, next_ref, x_ref, wg_ref, wv_ref, cg_ref, cv_ref, wd_ref, fin_ref,
                o_ref, hbuf, acc, *, tm, final_norm):
    i = pl.program_id(1)
    j = pl.program_id(2)
    ni = pl.num_programs(1)
    nj = pl.num_programs(2)

    @pl.when(j == 0)
    def _():
        zero = jnp.zeros((FFN_HALO, D_MODEL), BF16)
        hbuf[0:FFN_HALO, :] = jnp.where(i == 0, zero, prev_ref[0])
        hbuf[FFN_HALO:FFN_HALO + tm, :] = main_ref[0]
        hbuf[FFN_HALO + tm:, :] = jnp.where(i == ni - 1, zero, next_ref[0])
        acc[...] = jnp.zeros_like(acc)

    h = hbuf[...]
    rows = tm + 2 * FFN_HALO

    def conv(u, c_ref):
        um = pltpu.roll(u, 1, 0)
        up = pltpu.roll(u, rows - 1, 0)
        y = c_ref[0:1, :] * um + c_ref[1:2, :] * u + c_ref[2:3, :] * up
        return y[FFN_HALO:FFN_HALO + tm]

    gate = conv(_mm(h, wg_ref[...]), cg_ref)
    val = conv(_mm(h, wv_ref[...]), cv_ref)
    act = (gate * _sigmoid(gate) * val).astype(BF16)
    acc[...] += _mm(act, wd_ref[...])

    @pl.when(j == nj - 1)
    def _():
        xn = x_ref[0] + acc[...]
        if final_norm:
            ms = jnp.mean(xn * xn, axis=-1, keepdims=True)
            xn = xn * lax.rsqrt(ms + NORM_EPS) * fin_ref[...]
        o_ref[0] = xn


def _ffn(h3, x3, w_up, conv_w, w_down, fin_gain, final_norm, tm=512, tf=1408):
    B, S, _ = h3.shape
    tm = min(tm, S)
    nbh = tm // FFN_HALO
    lasth = S // FFN_HALO - 1
    nj = D_FF // tf
    wup = w_up.astype(BF16)
    wdn = w_down.astype(BF16)
    cw = jnp.pad(conv_w, ((0, 8 - conv_w.shape[0]), (0, 0)))
    return pl.pallas_call(
        functools.partial(_ffn_kernel, tm=tm, final_norm=final_norm),
        grid=(B, S // tm, nj),
        in_specs=[
            pl.BlockSpec((1, FFN_HALO, D_MODEL), lambda b, i, j: (b, jnp.maximum(i * nbh - 1, 0), 0)),
            pl.BlockSpec((1, tm, D_MODEL), lambda b, i, j: (b, i, 0)),
            pl.BlockSpec((1, FFN_HALO, D_MODEL),
                         lambda b, i, j: (b, jnp.minimum((i + 1) * nbh, lasth), 0)),
            pl.BlockSpec((1, tm, D_MODEL), lambda b, i, j: (b, i, 0)),
            pl.BlockSpec((D_MODEL, tf), lambda b, i, j: (0, j)),
            pl.BlockSpec((D_MODEL, tf), lambda b, i, j: (0, nj + j)),
            pl.BlockSpec((8, tf), lambda b, i, j: (0, j)),
            pl.BlockSpec((8, tf), lambda b, i, j: (0, nj + j)),
            pl.BlockSpec((tf, D_MODEL), lambda b, i, j: (j, 0)),
            pl.BlockSpec((1, D_MODEL), lambda b, i, j: (0, 0)),
        ],
        out_specs=pl.BlockSpec((1, tm, D_MODEL), lambda b, i, j: (b, i, 0)),
        out_shape=jax.ShapeDtypeStruct((B, S, D_MODEL), F32),
        scratch_shapes=[pltpu.VMEM((tm + 2 * FFN_HALO, D_MODEL), BF16),
                        pltpu.VMEM((tm, D_MODEL), F32)],
        compiler_params=_params(("parallel", "parallel", "arbitrary")),
        name="conv_ffn",
    )(h3, h3, h3, x3, wup, wup, cw, cw, wdn, fin_gain)


def _in_weights(w_in_l):
    a = ATT_WIDTH + 2 * ATT_KV_WIDTH
    d = a + 3 * DN_WIDTH
    z = d + DN_WIDTH
    g = z + 4 * DN_HEADS
    w_ab = jnp.pad(w_in_l[:, z:g], ((0, 0), (0, LANES - 4 * DN_HEADS)))
    groups = (w_in_l[:, :a], w_in_l[:, a:d], w_in_l[:, d:z], w_ab, w_in_l[:, g:])
    return [w.astype(BF16) for w in groups]


def kernel(x, positions, norm_mix, w_in, attn_sink, dn_conv, dn_a_log, dn_dt_bias, dn_norm, rw_mu, rw_w0, rw_w_up, rw_a0, rw_a_up, rw_g_up, rw_k_k, rw_k_a, rw_r_k, rw_ln_w, rw_ln_b, w_out, norm_ffn, ffn_w_up, ffn_conv, ffn_w_down, norm_final):
    B, S, D = x.shape
    T = B * S
    depth = w_in.shape[0]
    tables = _rope_tables(positions)
    x2 = x.reshape(T, D)
    for l in range(depth):
        att, dqkv, dz, dab, rwp = _in_proj(x2, norm_mix[l].reshape(1, D), _in_weights(w_in[l]), tables)
        y_att = _attention(att.reshape(B, S, ATT_OUT), attn_sink[l])
        qkv_act, gates = _dn_prep(dqkv.reshape(B, S, -1), dab.reshape(B, S, LANES),
                                  dn_conv[l], dn_a_log[l], dn_dt_bias[l])
        o_f, o_b = _dn_scan(qkv_act, gates)
        rw = _rw_prep(rwp.reshape(B, S, RW_IN), rw_mu[l], rw_w0[l], rw_w_up[l], rw_a0[l], rw_a_up[l],
                      rw_g_up[l], rw_k_k[l], rw_k_a[l], rw_r_k[l])
        r, v, g, bg = rw[:4]
        y_f, y_b = _rw_scan(r, v, (rw[4:8], rw[8:12]))
        flat = lambda a: a.reshape(T, a.shape[-1])
        x2, h2 = _out_proj(
            x2, flat(y_att), flat(o_f), flat(o_b), dz,
            jnp.tile(dn_norm[l], DN_HEADS).reshape(1, DN_WIDTH),
            flat(y_f), flat(y_b), flat(g), flat(bg),
            rw_ln_w[l].reshape(1, RW_WIDTH), rw_ln_b[l].reshape(1, RW_WIDTH),
            w_out[l], norm_ffn[l].reshape(1, D))
        x2 = _ffn(h2.reshape(B, S, D), x2.reshape(B, S, D), ffn_w_up[l], ffn_conv[l], ffn_w_down[l],
                  norm_final.reshape(1, D), final_norm=(l == depth - 1)).reshape(T, D)
    return x2.reshape(B, S, D)
```

```python
import functools

import jax
import jax.numpy as jnp
from jax import lax
from jax.experimental import pallas as pl
from jax.experimental.pallas import tpu as pltpu

F32 = jnp.float32
BF16 = jnp.bfloat16

D_MODEL = 1024
HEAD_DIM = 64
HEAD_SHIFT = 6
ATT_HEADS = 6
ATT_WIDTH = 384
ATT_KV_WIDTH = 128
WINDOW = 128
ROPE_DIM = 16
ROPE_THETA = 500000.0
DN_HEADS = 6
DN_WIDTH = 384
DN_CONV = 5
RW_HEADS = 4
RW_WIDTH = 256
W_RANK = 64
A_RANK = 64
G_RANK = 128
RW_IN = 1024
D_FF = 2816
NORM_EPS = 1e-6
RW_LN_EPS = 64e-5
LANES = 128
CHUNK = 64
NEG_BIG = -1e30

NN = (((1,), (0,)), ((), ()))
NT = (((1,), (1,)), ((), ()))
TN = (((0,), (0,)), ((), ()))

VMEM_LIMIT = 56 * 1024 * 1024


def _mm(a, b, dims=NN):
    return lax.dot_general(a, b, dims, preferred_element_type=F32)


def _split2(x):
    hi = x.astype(BF16)
    lo = (x - hi.astype(F32)).astype(BF16)
    return hi, lo


def _split3(x):
    hi = x.astype(BF16)
    r = x - hi.astype(F32)
    mid = r.astype(BF16)
    lo = (r - mid.astype(F32)).astype(BF16)
    return hi, mid, lo


def _mm1(a, b, dims=NN):
    return _mm(a.astype(BF16), b.astype(BF16), dims)


_mm_score = _mm1
_mm_neu_y = _mm1
_mm_neu_sq = _mm1
_mm_intra = _mm1
_mm_state = _mm1


def _mmx(a, e, dims=NN):
    h, l = _split2(a)
    return _mm(h, e, dims) + _mm(l, e, dims)


def _xmm(e, a, dims=NN):
    h, m, l = _split3(a)
    return _mm(e, h, dims) + (_mm(e, m, dims) + _mm(e, l, dims))


def _iota(shape, dim):
    return lax.broadcasted_iota(jnp.int32, shape, dim)


def _sigmoid(x):
    return 1.0 / (1.0 + jnp.exp(-x))


def _softplus(x):
    return jnp.maximum(x, 0.0) + jnp.log1p(jnp.exp(-jnp.abs(x)))


def _head_ones(width):
    r = _iota((width, width), 0) >> HEAD_SHIFT
    c = _iota((width, width), 1) >> HEAD_SHIFT
    return jnp.where(r == c, 1.0, 0.0).astype(BF16)


def _params(sem, vmem=VMEM_LIMIT):
    return pltpu.CompilerParams(dimension_semantics=sem, vmem_limit_bytes=vmem)


def _rope_table_kernel(pos_ref, freq_ref, c_ref, s1_ref, s2_ref):
    ang = pos_ref[...].astype(F32) * freq_ref[...]
    c = jnp.cos(ang)
    s = jnp.sin(ang)
    j = _iota(ang.shape, 1) & (ROPE_DIM - 1)
    half = ROPE_DIM // 2
    c_ref[...] = c
    s1_ref[...] = jnp.where(j < half, -s, 0.0)
    s2_ref[...] = jnp.where(j >= half, s, 0.0)


def _rope_tables(positions):
    T = positions.size
    half = ROPE_DIM // 2
    per_row = LANES // ROPE_DIM
    rows = T // per_row
    inv_freq = ROPE_THETA ** (-jnp.arange(half, dtype=F32) / half)
    freq = inv_freq[jnp.arange(LANES) % half][None, :]
    pos_c = jnp.repeat(positions.reshape(T), ROPE_DIM).reshape(rows, LANES)
    tm = min(rows, 1024)
    spec = pl.BlockSpec((tm, LANES), lambda i: (i, 0))
    compact = pl.pallas_call(
        _rope_table_kernel,
        grid=(rows // tm,),
        in_specs=[spec, pl.BlockSpec((1, LANES), lambda i: (0, 0))],
        out_specs=[spec, spec, spec],
        out_shape=[jax.ShapeDtypeStruct((rows, LANES), F32)] * 3,
        compiler_params=_params(("parallel",)),
        name="rope_tables",
    )(pos_c, freq)

    def expand(t, fill):
        t = t.reshape(T, ROPE_DIM)
        head = jnp.concatenate([t, jnp.full((T, HEAD_DIM - ROPE_DIM), fill, F32)], axis=1)
        return jnp.tile(head, (1, LANES // HEAD_DIM))

    return [expand(compact[0], 1.0), expand(compact[1], 0.0), expand(compact[2], 0.0)]


ATT_OUT = 7 * LANES


def _in_proj_kernel(x_ref, gain_ref, watt_ref, wdqkv_ref, wdz_ref, wab_ref, wrw_ref,
                    c_ref, s1_ref, s2_ref,
                    att_ref, dqkv_ref, dz_ref, dab_ref, rwp_ref):
    x = x_ref[...]
    ms = jnp.mean(x * x, axis=-1, keepdims=True)
    h = (x * lax.rsqrt(ms + NORM_EPS) * gain_ref[...]).astype(BF16)
    att = _mm(h, watt_ref[...])
    c, s1, s2 = c_ref[...], s1_ref[...], s2_ref[...]
    half = ROPE_DIM // 2
    for slab in range(4):
        t = att[:, slab * LANES:(slab + 1) * LANES]
        t = t * c + pltpu.roll(t, LANES - half, 1) * s1 + pltpu.roll(t, half, 1) * s2
        if slab < 3:
            att_ref[:, slab * LANES:(slab + 1) * LANES] = (t * (HEAD_DIM ** -0.5)).astype(BF16)
        else:
            att_ref[:, 3 * LANES:4 * LANES] = t.astype(BF16)
            att_ref[:, 5 * LANES:6 * LANES] = pltpu.roll(t, HEAD_DIM, 1).astype(BF16)
    v = att[:, 4 * LANES:5 * LANES]
    att_ref[:, 4 * LANES:5 * LANES] = v.astype(BF16)
    att_ref[:, 6 * LANES:7 * LANES] = pltpu.roll(v, HEAD_DIM, 1).astype(BF16)
    dqkv_ref[...] = _mm(h, wdqkv_ref[...])
    dz_ref[...] = _mm(h, wdz_ref[...])
    dab_ref[...] = _mm(h, wab_ref[...])
    rwp_ref[...] = _mm(h, wrw_ref[...])


def _in_proj(x2, gain, w, tables, tm=512):
    T = x2.shape[0]
    tm = min(tm, T)
    row = lambda wdt: pl.BlockSpec((tm, wdt), lambda i: (i, 0))
    full = lambda a: pl.BlockSpec(a.shape, lambda i: (0,) * a.ndim)
    widths = (ATT_OUT, 3 * DN_WIDTH, DN_WIDTH, LANES, RW_IN)
    dts = (BF16, F32, F32, F32, F32)
    return pl.pallas_call(
        _in_proj_kernel,
        grid=(T // tm,),
        in_specs=[row(D_MODEL), full(gain)] + [full(a) for a in w] + [row(LANES)] * 3,
        out_specs=[row(wd) for wd in widths],
        out_shape=[jax.ShapeDtypeStruct((T, wd), dt) for wd, dt in zip(widths, dts)],
        compiler_params=_params(("parallel",)),
        name="in_proj",
    )(x2, gain, *w, *tables)


def _attn_kernel(sink_ref, q_ref, *refs, tq, seq):
    kv_refs = refs[:12]
    o_ref = refs[12]
    win = refs[13:17]
    i = pl.program_id(1)
    for a in range(4):
        prev, main, nxt = kv_refs[3 * a:3 * a + 3]
        win[a][0:WINDOW, :] = prev[0]
        win[a][WINDOW:WINDOW + tq, :] = main[0]
        win[a][WINDOW + tq:, :] = nxt[0]
    kA, vA, kB, vB = win
    blk = WINDOW
    lane_hi = _iota((3 * blk, LANES), 1) >= HEAD_DIM
    qlane_hi = _iota((blk, LANES), 1) >= HEAD_DIM
    qi = _iota((blk, 3 * blk), 0)
    kk = _iota((blk, 3 * blk), 1)
    rel = kk - blk - qi
    band = (rel <= WINDOW) & (rel >= -WINDOW)
    for s in range(tq // blk):
        r0 = s * blk
        abs_k = i * tq + r0 + kk - blk
        ok = band & (abs_k >= 0) & (abs_k < seq)
        kwin = {0: kA[r0:r0 + 3 * blk, :], 1: kB[r0:r0 + 3 * blk, :]}
        vwin = {0: vA[r0:r0 + 3 * blk, :], 1: vB[r0:r0 + 3 * blk, :]}
        src = [0 if (h // 3) == (h % 2) else 1 for h in range(ATT_HEADS)]
        scores = []
        for h in range(ATT_HEADS):
            qp = q_ref[0, r0:r0 + blk, (h // 2) * LANES:(h // 2 + 1) * LANES]
            is_half = qlane_hi if h % 2 == 1 else jnp.logical_not(qlane_hi)
            qm = jnp.where(is_half, qp, jnp.zeros_like(qp))
            scores.append(_mm(qm, kwin[src[h]], NT))
        probs, inv = [], []
        for h in range(ATT_HEADS):
            sc = jnp.where(ok, scores[h], NEG_BIG)
            sk = sink_ref[h]
            m = jnp.maximum(jnp.max(sc, axis=-1, keepdims=True), sk)
            pexp = jnp.exp(sc - m)
            den = jnp.sum(pexp, axis=-1, keepdims=True) + jnp.exp(sk - m)
            probs.append(pexp.astype(BF16))
            inv.append(1.0 / den)
        for p in range(ATT_HEADS // 2):
            acc = jnp.zeros((blk, LANES), F32)
            for half in range(2):
                h = 2 * p + half
                v_half = lane_hi if half == 1 else jnp.logical_not(lane_hi)
                vm = jnp.where(v_half, vwin[src[h]], jnp.zeros_like(vwin[src[h]]))
                acc = acc + _mm(probs[h], vm) * inv[h]
            o_ref[0, r0:r0 + blk, p * LANES:(p + 1) * LANES] = acc.astype(o_ref.dtype)


def _attention(att3, sink, tq=512):
    B, S, _ = att3.shape
    tq = min(tq, S)
    nb = tq // WINDOW
    last = S // WINDOW - 1
    specs = [pl.BlockSpec(memory_space=pltpu.SMEM),
             pl.BlockSpec((1, tq, ATT_WIDTH), lambda b, i: (b, i, 0))]
    for col in (3, 4, 5, 6):
        specs.append(pl.BlockSpec((1, WINDOW, LANES),
                                  lambda b, i, col=col: (b, jnp.maximum(i * nb - 1, 0), col)))
        specs.append(pl.BlockSpec((1, tq, LANES), lambda b, i, col=col: (b, i, col)))
        specs.append(pl.BlockSpec((1, WINDOW, LANES),
                                  lambda b, i, col=col: (b, jnp.minimum((i + 1) * nb, last), col)))
    return pl.pallas_call(
        functools.partial(_attn_kernel, tq=tq, seq=S),
        grid=(B, S // tq),
        in_specs=specs,
        out_specs=pl.BlockSpec((1, tq, ATT_WIDTH), lambda b, i: (b, i, 0)),
        out_shape=jax.ShapeDtypeStruct((B, S, ATT_WIDTH), BF16),
        scratch_shapes=[pltpu.VMEM((tq + 2 * WINDOW, LANES), BF16)] * 4,
        compiler_params=_params(("parallel", "parallel")),
        name="window_attention",
    )(sink, att3, *([att3] * 12))


def _cumsum_mats():
    rc = _iota((CHUNK, CHUNK), 0)
    cc = _iota((CHUNK, CHUNK), 1)
    return (jnp.where(cc <= rc, 1.0, 0.0).astype(BF16), jnp.where(cc >= rc, 1.0, 0.0).astype(BF16))


def _chunk_consts(reverse):
    C = CHUNK
    ri = _iota((C, LANES), 0)
    li = _iota((C, LANES), 1)
    j = li & (HEAD_DIM - 1)
    r2 = _iota((LANES, LANES), 0)
    c2 = _iota((LANES, LANES), 1)
    if reverse:
        incl, strict = j >= ri, j > ri
    else:
        incl, strict = j <= ri, j < ri
    return dict(
        incl=incl, strict=strict, hi=li >= HEAD_DIM,
        bdmask=(r2 >= HEAD_DIM) == (c2 >= HEAD_DIM),
        tot_row=0 if reverse else C - 1,
    )


def _bd(y):
    hi = (_iota(y.shape, 1) & (LANES - 1)) >= HEAD_DIM
    zero = jnp.zeros_like(y)
    return jnp.concatenate([jnp.where(hi, zero, y), jnp.where(hi, y, zero)], axis=0)


def _apply(x2, y, mm):
    return mm(x2, _bd(y))


def _xsel(a, e):
    h, m, l = _split3(a)
    return _mm(h, e) + (_mm(m, e) + _mm(l, e))


def _gate_select(base):
    r = _iota((LANES, 2 * LANES), 0)
    c = _iota((LANES, 2 * LANES), 1)
    src = base + ((c >> HEAD_SHIFT) & 1) + jnp.where(c >= LANES, 2 * DN_HEADS, 0)
    return jnp.where(r == src, 1.0, 0.0).astype(BF16)


_NEED_STATE = object()


def _neumann(n2, y):
    eye = (_iota(n2.shape, 1) & (HEAD_DIM - 1)) == _iota(n2.shape, 0)
    t = n2 + jnp.where(eye, 1.0, 0.0)
    pw = _apply(n2, n2, _mm_neu_sq)
    yield
    levels = CHUNK.bit_length() - 2
    for _ in range(levels):
        both = _apply(pw, jnp.concatenate([t, pw], axis=1), _mm_neu_sq)
        t = t + both[:, :LANES]
        pw = both[:, LANES:]
        yield
    return _apply(t, y, _mm_neu_y)


def _interleave(gens, first_state, pred):
    n = len(gens)
    results = [None] * n
    waiting = [False] * n
    done = [False] * n
    while not all(done):
        for i in range(n):
            if done[i]:
                continue
            try:
                if waiting[i]:
                    if pred[i] is None:
                        state = first_state[i]
                    elif done[pred[i]]:
                        state = results[pred[i]][1]
                    else:
                        continue
                    waiting[i] = False
                    token = gens[i].send(state)
                else:
                    token = next(gens[i])
                if token is _NEED_STATE:
                    waiting[i] = True
            except StopIteration as stop:
                results[i] = stop.value
                done[i] = True
    return results


def _dn_chunk(q2, k2, v2, gates, base, gc_row, cs):
    C = CHUNK
    gx = _xsel(gates, _gate_select(base))
    yield
    gcb, bb = gx[:, :LANES], gx[:, LANES:]
    kb = k2 * bb
    vb = v2 * bb
    s2 = _mm_score(jnp.concatenate([kb, q2], axis=0), _bd(k2), NT)
    row = cs["tot_row"]
    gtot = jnp.broadcast_to(gcb[row:row + 1, :], (C, LANES))
    eg = jnp.exp(gcb)
    kbe = kb * eg
    qg = q2 * eg
    kg = k2 * jnp.exp(gtot - gcb)
    rj = jnp.broadcast_to(gc_row, (C, LANES))
    yield
    dec = jnp.exp(jnp.where(cs["incl"], gcb - rj, NEG_BIG))
    l2 = jnp.where(cs["strict"], s2[:C] * dec, 0.0)
    qk = s2[C:] * dec
    y = yield from _neumann(-l2, jnp.concatenate([vb, kbe], axis=1))
    value, kcum = y[:, :LANES], y[:, LANES:]
    return value, kcum, qg, kg, qk, gtot


def _dn_state(load, cs):
    C = CHUNK
    P = yield _NEED_STATE
    sp = _mm_state(jnp.concatenate([load(1), load(2)], axis=0), P)
    yield
    vnew = load(0) - sp[:C]
    o = sp[C:] + _apply(load(4), vnew, _mm_state)
    upd = _mm_state(load(3), vnew, TN)
    yield
    gtot = load(5)
    glast = jnp.exp(jnp.concatenate([gtot, gtot], axis=0))
    pn = P * glast + jnp.where(cs["bdmask"], upd, 0.0)
    return o, pn


def _rw_chunk(r2, v2, at, gcb, kd2, b2, cs):
    C = CHUNK
    row = cs["tot_row"]
    gtot = jnp.broadcast_to(gcb[row:row + 1, :], (C, LANES))
    eneg = jnp.exp(-gcb)
    bt = b2 * eneg
    kt = kd2 * eneg
    rt = r2 * jnp.exp(gcb)
    er = jnp.exp(gtot - gcb)
    bh = b2 * er
    kh = kd2 * er
    bk = jnp.concatenate([_bd(bt), _bd(kt)], axis=0)
    sa = _mm_score(at, bk, NT)
    sr = _mm_score(rt, bk, NT)
    yield
    aab = jnp.where(cs["strict"], sa[:, :LANES], 0.0)
    aak = jnp.where(cs["strict"], sa[:, LANES:], 0.0)
    arb = jnp.where(cs["incl"], sr[:, :LANES], 0.0)
    ark = jnp.where(cs["incl"], sr[:, LANES:], 0.0)
    av = _apply(aak, v2, _mm_intra)
    arkv = _apply(ark, v2, _mm_intra)
    yield
    y = yield from _neumann(aab, jnp.concatenate([av, at], axis=1))
    vp, ap = y[:, :LANES], y[:, LANES:]
    return vp, ap, rt, arb, arkv, bh, kh, v2, gtot


def _rw_state(load, cs):
    C = CHUNK
    P = yield _NEED_STATE
    sp = _mm_state(jnp.concatenate([load(1), load(2)], axis=0), P)
    yield
    u = sp[:C] + load(0)
    v2 = load(7)
    out = sp[C:] + _apply(load(3), u, _mm_state) + load(4)
    upd = _mm_state(jnp.concatenate([load(5), load(6)], axis=0), jnp.concatenate([u, v2], axis=0), TN)
    yield
    gtot = load(8)
    wtot = jnp.exp(jnp.concatenate([gtot, gtot], axis=0))
    pn = P * wtot.T + jnp.where(cs["bdmask"], upd, 0.0)
    return out, pn


def _dn_prep_kernel(prev_ref, main_ref, next_ref, dab_ref, cw_ref, alog_ref, dt_ref,
                    qkv_ref, gates_ref, xbuf, *, tm):
    i = pl.program_id(1)
    n = pl.num_programs(1)
    halo = 8
    pad = (DN_CONV - 1) // 2
    xbuf[0:halo, :] = jnp.where(i == 0, 0.0, prev_ref[0])
    xbuf[halo:halo + tm, :] = main_ref[0]
    xbuf[halo + tm:, :] = jnp.where(i == n - 1, 0.0, next_ref[0])
    ones = _head_ones(LANES)
    for slab in range(3 * DN_WIDTH // LANES):
        cols = slice(slab * LANES, (slab + 1) * LANES)
        y = jnp.zeros((tm, LANES), F32)
        for j in range(DN_CONV):
            y = y + cw_ref[j:j + 1, cols] * xbuf[halo - pad + j:halo - pad + j + tm, cols]
        y = y * _sigmoid(y)
        if slab < 6:
            ss = _mmx(y * y, ones)
            y = y * lax.rsqrt(ss + 1e-6)
        if slab < 3:
            y = y * (HEAD_DIM ** -0.5)
        qkv_ref[0, :, cols] = y
    ab = dab_ref[0]
    lane = _iota((CHUNK, LANES), 1)
    g = -jnp.exp(alog_ref[...]) * _softplus(ab + dt_ref[...])
    beta = _sigmoid(ab)
    tm_f, tm_b = _cumsum_mats()
    for c in range(tm // CHUNK):
        rows = slice(c * CHUNK, (c + 1) * CHUNK)
        gc = jnp.where(lane < DN_HEADS, _xmm(tm_f, g[rows]), _xmm(tm_b, g[rows]))
        gates_ref[0, rows, :] = jnp.where(lane < 2 * DN_HEADS, gc, beta[rows])


def _dn_prep(dqkv3, dab3, conv_w, a_log, dt_bias, tm=512):
    B, S, W = dqkv3.shape
    tm = min(tm, S)
    nb8 = tm // 8
    last8 = S // 8 - 1
    cw = jnp.pad(conv_w, ((0, 8 - DN_CONV), (0, 0)))
    pad_row = lambda a: jnp.pad(a.reshape(1, -1), ((0, 0), (0, LANES - a.size)))
    small = lambda a: pl.BlockSpec(a.shape, lambda b, i: (0,) * a.ndim)
    alog, dt = pad_row(a_log), pad_row(dt_bias)
    return pl.pallas_call(
        functools.partial(_dn_prep_kernel, tm=tm),
        grid=(B, S // tm),
        in_specs=[pl.BlockSpec((1, 8, W), lambda b, i: (b, jnp.maximum(i * nb8 - 1, 0), 0)),
                  pl.BlockSpec((1, tm, W), lambda b, i: (b, i, 0)),
                  pl.BlockSpec((1, 8, W), lambda b, i: (b, jnp.minimum((i + 1) * nb8, last8), 0)),
                  pl.BlockSpec((1, tm, LANES), lambda b, i: (b, i, 0)),
                  small(cw), small(alog), small(dt)],
        out_specs=[pl.BlockSpec((1, tm, W), lambda b, i: (b, i, 0)),
                   pl.BlockSpec((1, tm, LANES), lambda b, i: (b, i, 0))],
        out_shape=[jax.ShapeDtypeStruct((B, S, W), F32),
                   jax.ShapeDtypeStruct((B, S, LANES), F32)],
        scratch_shapes=[pltpu.VMEM((tm + 16, W), F32)],
        compiler_params=_params(("parallel", "parallel")),
        name="deltanet_prep",
    )(dqkv3, dqkv3, dqkv3, dab3, cw, alog, dt)


def _scan_step(state, insts, second):
    def whole(gen, cs, write):
        mid = yield from gen
        o, pn = yield from second(lambda a: mid[a], cs)
        write(o)
        return None, pn

    gens, pred, first = [], [], []
    last = {}
    for i, (chain, gen, cs, write) in enumerate(insts):
        gens.append(whole(gen, cs, write))
        pred.append(last.get(chain))
        first.append(None if chain in last else state[chain])
        last[chain] = i
    results = _interleave(gens, first, pred)
    for chain, i in last.items():
        state[chain] = results[i][1]


def _scan_reset(state):
    @pl.when(pl.program_id(1) == 0)
    def _():
        state[...] = jnp.zeros_like(state)


def _dn_scan_kernel(qf_ref, gf_ref, tf_ref, qb_ref, gb_ref, tb_ref, of_ref, ob_ref, state, *, cpb):
    _scan_reset(state)
    pairs = DN_HEADS // 2
    consts = [_chunk_consts(reverse=False), _chunk_consts(reverse=True)]
    insts = []
    for step in range(cpb):
        for d, (q_ref, g_ref, t_ref, o_ref) in enumerate(((qf_ref, gf_ref, tf_ref, of_ref),
                                                          (qb_ref, gb_ref, tb_ref, ob_ref))):
            c = step if d == 0 else cpb - 1 - step
            rows = slice(c * CHUNK, (c + 1) * CHUNK)
            gates = g_ref[0, rows, :]
            gc_rows = t_ref[0, c]
            for p in range(pairs):
                cols = slice(p * LANES, (p + 1) * LANES)
                q2 = q_ref[0, rows, p * LANES:(p + 1) * LANES]
                k2 = q_ref[0, rows, DN_WIDTH + p * LANES:DN_WIDTH + (p + 1) * LANES]
                v2 = q_ref[0, rows, 2 * DN_WIDTH + p * LANES:2 * DN_WIDTH + (p + 1) * LANES]
                gc_row = gc_rows[d * pairs + p:d * pairs + p + 1, :]
                gen = _dn_chunk(q2, k2, v2, gates, d * DN_HEADS + 2 * p, gc_row, consts[d])

                def write(o, o_ref=o_ref, rows=rows, cols=cols):
                    o_ref[0, rows, cols] = o

                insts.append(((d, p), gen, consts[d], write))
    _scan_step(state, insts, _dn_state)


def _scan_maps(nblk, ndim):
    tail = (0,) * (ndim - 2)
    return (lambda b, n: (b, n) + tail, lambda b, n: (b, nblk - 1 - n) + tail)


def _dn_scan(qkv3, gates3, cpb=8):
    B, S, W = qkv3.shape
    blk = cpb * CHUNK
    nblk = S // blk
    nchunk = S // CHUNK
    pairs = DN_HEADS // 2
    gct = gates3[..., :2 * DN_HEADS].reshape(B, nchunk, CHUNK, 2, pairs, 2)
    gct = gct.transpose(0, 1, 3, 4, 5, 2).reshape(B, nchunk, 2 * pairs, LANES)
    gct = jnp.pad(gct, ((0, 0), (0, 0), (0, 8 - 2 * pairs), (0, 0)))
    fwd, bwd = _scan_maps(nblk, 3)
    fwd4, bwd4 = _scan_maps(nblk, 4)
    return pl.pallas_call(
        functools.partial(_dn_scan_kernel, cpb=cpb),
        grid=(B, nblk),
        in_specs=[pl.BlockSpec((1, blk, W), fwd), pl.BlockSpec((1, blk, LANES), fwd),
                  pl.BlockSpec((1, cpb, 8, LANES), fwd4),
                  pl.BlockSpec((1, blk, W), bwd), pl.BlockSpec((1, blk, LANES), bwd),
                  pl.BlockSpec((1, cpb, 8, LANES), bwd4)],
        out_specs=[pl.BlockSpec((1, blk, DN_WIDTH), fwd), pl.BlockSpec((1, blk, DN_WIDTH), bwd)],
        out_shape=[jax.ShapeDtypeStruct((B, S, DN_WIDTH), F32)] * 2,
        scratch_shapes=[pltpu.VMEM((2, DN_HEADS // 2, LANES, LANES), F32)],
        compiler_params=_params(("parallel", "arbitrary")),
        name="deltanet_scan",
    )(qkv3, gates3, gct, qkv3, gates3, gct)


RW_OUTS = 12


def _rw_prep_kernel(prev_ref, main_ref, next_ref, mu_ref, w0_ref, a0_ref, kk_ref, ka_ref, rk_ref,
                    wup_ref, aup_ref, gup_ref, *refs, tm):
    outs = refs[:RW_OUTS]
    xbuf, pbuf = refs[RW_OUTS:]
    i = pl.program_id(1)
    n = pl.num_programs(1)
    halo = 8
    xbuf[0:halo, :] = jnp.where(i == 0, 0.0, prev_ref[0])
    xbuf[halo:halo + tm, :] = main_ref[0]
    xbuf[halo + tm:, :] = jnp.where(i == n - 1, 0.0, next_ref[0])
    for slab in range(RW_IN // LANES):
        cols = slice(slab * LANES, (slab + 1) * LANES)
        cur = xbuf[halo:halo + tm, cols]
        prv = xbuf[halo - 1:halo - 1 + tm, cols]
        nxt = xbuf[halo + 1:halo + 1 + tm, cols]
        pbuf[:, cols] = cur + mu_ref[0:1, cols] * (prv - cur) + mu_ref[1:2, cols] * (nxt - cur)
    r = pbuf[:, 0:RW_WIDTH]
    k = pbuf[:, RW_WIDTH:2 * RW_WIDTH]
    v = pbuf[:, 2 * RW_WIDTH:3 * RW_WIDTH]
    low = pbuf[:, 3 * RW_WIDTH:3 * RW_WIDTH + LANES]
    gd = pbuf[:, 3 * RW_WIDTH + LANES:]
    ones = _head_ones(RW_WIDTH)
    r_o, v_o, g_o, bg_o = outs[:4]
    g = _mm1(_sigmoid(gd), gup_ref[...])
    kq = k * kk_ref[...]
    kk = kq * lax.rsqrt(_mmx(kq * kq, ones) + 1e-6)
    bonus = _mmx(r * k * rk_ref[...], ones) * v
    r_o[0] = r
    v_o[0] = v
    g_o[0] = g
    bg_o[0] = bonus * g
    wl = jnp.tanh(low)
    cum = _cumsum_mats()
    for d in range(2):
        at_o, gc_o, kd_o, b_o = outs[4 + 4 * d:8 + 4 * d]
        w = -_softplus(-(w0_ref[d:d + 1, :] + _mm1(wl, wup_ref[d]))) - 0.5
        lw = -jnp.exp(w)
        for c in range(tm // CHUNK):
            rows = slice(c * CHUNK, (c + 1) * CHUNK)
            gc = _xmm(cum[d], lw[rows])
            gc_o[0, rows, :] = gc
            at_o[0, rows, :] = -kk[rows] * jnp.exp(gc - lw[rows])
        a = _sigmoid(a0_ref[d:d + 1, :] + _mm1(low, aup_ref[d]))
        kd_o[0] = k * (1.0 + (a - 1.0) * ka_ref[...])
        b_o[0] = kk * a


def _rw_prep(rwp3, mu, w0, w_up, a0, a_up, g_up, k_k, k_a, r_k, tm=512):
    B, S, W = rwp3.shape
    tm = min(tm, S)
    nb8 = tm // 8
    last8 = S // 8 - 1
    zeros = jnp.zeros((2, W_RANK, RW_WIDTH), F32)
    wup = jnp.concatenate([w_up, zeros], axis=1)
    aup = jnp.concatenate([zeros, a_up], axis=1)
    rowv = lambda a: a.reshape(1, RW_WIDTH)
    small = lambda a: pl.BlockSpec(a.shape, lambda b, i: (0,) * a.ndim)
    params = [mu, w0, a0, rowv(k_k), rowv(k_a), rowv(r_k), wup, aup, g_up]
    tile = pl.BlockSpec((1, tm, RW_WIDTH), lambda b, i: (b, i, 0))
    return pl.pallas_call(
        functools.partial(_rw_prep_kernel, tm=tm),
        grid=(B, S // tm),
        in_specs=[pl.BlockSpec((1, 8, W), lambda b, i: (b, jnp.maximum(i * nb8 - 1, 0), 0)),
                  pl.BlockSpec((1, tm, W), lambda b, i: (b, i, 0)),
                  pl.BlockSpec((1, 8, W), lambda b, i: (b, jnp.minimum((i + 1) * nb8, last8), 0))]
                 + [small(a) for a in params],
        out_specs=[tile] * RW_OUTS,
        out_shape=[jax.ShapeDtypeStruct((B, S, RW_WIDTH), F32)] * RW_OUTS,
        scratch_shapes=[pltpu.VMEM((tm + 16, W), F32), pltpu.VMEM((tm, W), F32)],
        compiler_params=_params(("parallel", "parallel")),
        name="rwkv_prep",
    )(rwp3, rwp3, rwp3, *params)


def _rw_scan_kernel(*refs, cpb):
    ins = refs[:12]
    yf_ref, yb_ref, state = refs[12:]
    _scan_reset(state)
    pairs = RW_HEADS // 2
    consts = [_chunk_consts(reverse=False), _chunk_consts(reverse=True)]
    insts = []
    for step in range(cpb):
        for d, o_ref in enumerate((yf_ref, yb_ref)):
            src = ins[6 * d:6 * d + 6]
            c = step if d == 0 else cpb - 1 - step
            rows = slice(c * CHUNK, (c + 1) * CHUNK)
            for p in range(pairs):
                cols = slice(p * LANES, (p + 1) * LANES)
                gen = _rw_chunk(*[ref[0, rows, cols] for ref in src], consts[d])

                def write(o, o_ref=o_ref, rows=rows, cols=cols):
                    o_ref[0, rows, cols] = o

                insts.append(((d, p), gen, consts[d], write))
    _scan_step(state, insts, _rw_state)


def _rw_scan(r, v, dirs, cpb=8):
    B, S, W = r.shape
    blk = cpb * CHUNK
    nblk = S // blk
    fwd, bwd = [pl.BlockSpec((1, blk, W), m) for m in _scan_maps(nblk, 3)]
    return pl.pallas_call(
        functools.partial(_rw_scan_kernel, cpb=cpb),
        grid=(B, nblk),
        in_specs=[fwd] * 6 + [bwd] * 6,
        out_specs=[fwd, bwd],
        out_shape=[jax.ShapeDtypeStruct((B, S, W), F32)] * 2,
        scratch_shapes=[pltpu.VMEM((2, RW_HEADS // 2, LANES, LANES), F32)],
        compiler_params=_params(("parallel", "arbitrary")),
        name="rwkv_scan",
    )(r, v, *dirs[0], r, v, *dirs[1])


def _out_proj_kernel(x_ref, ya_ref, of_ref, ob_ref, z_ref, dnw_ref, yf_ref, yb_ref, g_ref, bg_ref,
                     lnw_ref, lnb_ref, wa_ref, wd_ref, wr_ref, gain_ref, xo_ref, h_ref):
    o = of_ref[...] + ob_ref[...]
    ms = _mmx(o * o, _head_ones(DN_WIDTH)) * (1.0 / HEAD_DIM)
    z = z_ref[...]
    ydn = o * lax.rsqrt(ms + NORM_EPS) * dnw_ref[...] * (z * _sigmoid(z))
    y = yf_ref[...] + yb_ref[...]
    ones = _head_ones(RW_WIDTH)
    mean = _mmx(y, ones) * (1.0 / HEAD_DIM)
    yc = y - mean
    var = _mmx(yc * yc, ones) * (1.0 / HEAD_DIM)
    yrw = (yc * lax.rsqrt(var + RW_LN_EPS) * lnw_ref[...] + lnb_ref[...]) * g_ref[...] + bg_ref[...]
    acc = _mm(ya_ref[...], wa_ref[...])
    acc = acc + _mm(ydn.astype(BF16), wd_ref[...])
    acc = acc + _mm(yrw.astype(BF16), wr_ref[...])
    xn = x_ref[...] + acc
    xo_ref[...] = xn
    ms2 = jnp.mean(xn * xn, axis=-1, keepdims=True)
    h_ref[...] = (xn * lax.rsqrt(ms2 + NORM_EPS) * gain_ref[...]).astype(BF16)


def _out_proj(x2, ya, of, ob, z, dnw, yf, yb, g, bg, lnw, lnb, w_out, gain, tm=512):
    T = x2.shape[0]
    tm = min(tm, T)
    row = lambda a: pl.BlockSpec((tm, a.shape[1]), lambda i: (i, 0))
    full = lambda a: pl.BlockSpec(a.shape, lambda i: (0,) * a.ndim)
    wa = w_out[:ATT_WIDTH].astype(BF16)
    wd = w_out[ATT_WIDTH:ATT_WIDTH + DN_WIDTH].astype(BF16)
    wr = w_out[ATT_WIDTH + DN_WIDTH:].astype(BF16)
    args = [x2, ya, of, ob, z, dnw, yf, yb, g, bg, lnw, lnb, wa, wd, wr, gain]
    is_row = [True, True, True, True, True, False, True, True, True, True, False, False,
              False, False, False, False]
    return pl.pallas_call(
        _out_proj_kernel,
        grid=(T // tm,),
        in_specs=[row(a) if r else full(a) for a, r in zip(args, is_row)],
        out_specs=[pl.BlockSpec((tm, D_MODEL), lambda i: (i, 0))] * 2,
        out_shape=[jax.ShapeDtypeStruct((T, D_MODEL), F32), jax.ShapeDtypeStruct((T, D_MODEL), BF16)],
        compiler_params=_params(("parallel",)),
        name="out_proj",
    )(*args)


FFN_HALO = 16


def _ffn_kernel(prev_ref, main_ref, next_ref, x_ref, wg_ref, wv_ref, cg_ref, cv_ref, wd_ref, fin_ref,
                o_ref, hbuf, acc, *, tm, final_norm):
    i = pl.program_id(1)
    j = pl.program_id(2)
    ni = pl.num_programs(1)
    nj = pl.num_programs(2)

    @pl.when(j == 0)
    def _():
        zero = jnp.zeros((FFN_HALO, D_MODEL), BF16)
        hbuf[0:FFN_HALO, :] = jnp.where(i == 0, zero, prev_ref[0])
        hbuf[FFN_HALO:FFN_HALO + tm, :] = main_ref[0]
        hbuf[FFN_HALO + tm:, :] = jnp.where(i == ni - 1, zero, next_ref[0])
        acc[...] = jnp.zeros_like(acc)

    h = hbuf[...]
    rows = tm + 2 * FFN_HALO

    def conv(u, c_ref):
        um = pltpu.roll(u, 1, 0)
        up = pltpu.roll(u, rows - 1, 0)
        y = c_ref[0:1, :] * um + c_ref[1:2, :] * u + c_ref[2:3, :] * up
        return y[FFN_HALO:FFN_HALO + tm]

    gate = conv(_mm(h, wg_ref[...]), cg_ref)
    val = conv(_mm(h, wv_ref[...]), cv_ref)
    act = (gate * _sigmoid(gate) * val).astype(BF16)
    acc[...] += _mm(act, wd_ref[...])

    @pl.when(j == nj - 1)
    def _():
        xn = x_ref[0] + acc[...]
        if final_norm:
            ms = jnp.mean(xn * xn, axis=-1, keepdims=True)
            xn = xn * lax.rsqrt(ms + NORM_EPS) * fin_ref[...]
        o_ref[0] = xn


def _ffn(h3, x3, w_up, conv_w, w_down, fin_gain, final_norm, tm=512, tf=1408):
    B, S, _ = h3.shape
    tm = min(tm, S)
    nbh = tm // FFN_HALO
    lasth = S // FFN_HALO - 1
    nj = D_FF // tf
    wup = w_up.astype(BF16)
    wdn = w_down.astype(BF16)
    cw = jnp.pad(conv_w, ((0, 8 - conv_w.shape[0]), (0, 0)))
    return pl.pallas_call(
        functools.partial(_ffn_kernel, tm=tm, final_norm=final_norm),
        grid=(B, S // tm, nj),
        in_specs=[
            pl.BlockSpec((1, FFN_HALO, D_MODEL), lambda b, i, j: (b, jnp.maximum(i * nbh - 1, 0), 0)),
            pl.BlockSpec((1, tm, D_MODEL), lambda b, i, j: (b, i, 0)),
            pl.BlockSpec((1, FFN_HALO, D_MODEL),
                         lambda b, i, j: (b, jnp.minimum((i + 1) * nbh, lasth), 0)),
            pl.BlockSpec((1, tm, D_MODEL), lambda b, i, j: (b, i, 0)),
            pl.BlockSpec((D_MODEL, tf), lambda b, i, j: (0, j)),
            pl.BlockSpec((D_MODEL, tf), lambda b, i, j: (0, nj + j)),
            pl.BlockSpec((8, tf), lambda b, i, j: (0, j)),
            pl.BlockSpec((8, tf), lambda b, i, j: (0, nj + j)),
            pl.BlockSpec((tf, D_MODEL), lambda b, i, j: (j, 0)),
            pl.BlockSpec((1, D_MODEL), lambda b, i, j: (0, 0)),
        ],
        out_specs=pl.BlockSpec((1, tm, D_MODEL), lambda b, i, j: (b, i, 0)),
        out_shape=jax.ShapeDtypeStruct((B, S, D_MODEL), F32),
        scratch_shapes=[pltpu.VMEM((tm + 2 * FFN_HALO, D_MODEL), BF16),
                        pltpu.VMEM((tm, D_MODEL), F32)],
        compiler_params=_params(("parallel", "parallel", "arbitrary")),
        name="conv_ffn",
    )(h3, h3, h3, x3, wup, wup, cw, cw, wdn, fin_gain)


def _in_weights(w_in_l):
    a = ATT_WIDTH + 2 * ATT_KV_WIDTH
    d = a + 3 * DN_WIDTH
    z = d + DN_WIDTH
    g = z + 4 * DN_HEADS
    w_ab = jnp.pad(w_in_l[:, z:g], ((0, 0), (0, LANES - 4 * DN_HEADS)))
    groups = (w_in_l[:, :a], w_in_l[:, a:d], w_in_l[:, d:z], w_ab, w_in_l[:, g:])
    return [w.astype(BF16) for w in groups]


def kernel(x, positions, norm_mix, w_in, attn_sink, dn_conv, dn_a_log, dn_dt_bias, dn_norm, rw_mu, rw_w0, rw_w_up, rw_a0, rw_a_up, rw_g_up, rw_k_k, rw_k_a, rw_r_k, rw_ln_w, rw_ln_b, w_out, norm_ffn, ffn_w_up, ffn_conv, ffn_w_down, norm_final):
    B, S, D = x.shape
    T = B * S
    depth = w_in.shape[0]
    tables = _rope_tables(positions)
    x2 = x.reshape(T, D)
    for l in range(depth):
        att, dqkv, dz, dab, rwp = _in_proj(x2, norm_mix[l].reshape(1, D), _in_weights(w_in[l]), tables)
        y_att = _attention(att.reshape(B, S, ATT_OUT), attn_sink[l])
        qkv_act, gates = _dn_prep(dqkv.reshape(B, S, -1), dab.reshape(B, S, LANES),
                                  dn_conv[l], dn_a_log[l], dn_dt_bias[l])
        o_f, o_b = _dn_scan(qkv_act, gates)
        rw = _rw_prep(rwp.reshape(B, S, RW_IN), rw_mu[l], rw_w0[l], rw_w_up[l], rw_a0[l], rw_a_up[l],
                      rw_g_up[l], rw_k_k[l], rw_k_a[l], rw_r_k[l])
        r, v, g, bg = rw[:4]
        y_f, y_b = _rw_scan(r, v, (rw[4:8], rw[8:12]))
        flat = lambda a: a.reshape(T, a.shape[-1])
        x2, h2 = _out_proj(
            x2, flat(y_att), flat(o_f), flat(o_b), dz,
            jnp.tile(dn_norm[l], DN_HEADS).reshape(1, DN_WIDTH),
            flat(y_f), flat(y_b), flat(g), flat(bg),
            rw_ln_w[l].reshape(1, RW_WIDTH), rw_ln_b[l].reshape(1, RW_WIDTH),
            w_out[l], norm_ffn[l].reshape(1, D))
        x2 = _ffn(h2.reshape(B, S, D), x2.reshape(B, S, D), ffn_w_up[l], ffn_conv[l], ffn_w_down[l],
                  norm_final.reshape(1, D), final_norm=(l == depth - 1)).reshape(T, D)
    return x2.reshape(B, S, D)
```

```python
import functools

import jax
import jax.numpy as jnp
from jax import lax
from jax.experimental import pallas as pl
from jax.experimental.pallas import tpu as pltpu

F32 = jnp.float32
BF16 = jnp.bfloat16

D_MODEL = 1024
HEAD_DIM = 64
HEAD_SHIFT = 6
ATT_HEADS = 6
ATT_WIDTH = 384
ATT_KV_WIDTH = 128
WINDOW = 128
ROPE_DIM = 16
ROPE_THETA = 500000.0
DN_HEADS = 6
DN_WIDTH = 384
DN_CONV = 5
RW_HEADS = 4
RW_WIDTH = 256
W_RANK = 64
A_RANK = 64
G_RANK = 128
RW_IN = 1024
D_FF = 2816
NORM_EPS = 1e-6
RW_LN_EPS = 64e-5
LANES = 128
CHUNK = 64
NEG_BIG = -1e30

NN = (((1,), (0,)), ((), ()))
NT = (((1,), (1,)), ((), ()))
TN = (((0,), (0,)), ((), ()))

VMEM_LIMIT = 56 * 1024 * 1024


def _mm(a, b, dims=NN):
    return lax.dot_general(a, b, dims, preferred_element_type=F32)


def _split2(x):
    hi = x.astype(BF16)
    lo = (x - hi.astype(F32)).astype(BF16)
    return hi, lo


def _split3(x):
    hi = x.astype(BF16)
    r = x - hi.astype(F32)
    mid = r.astype(BF16)
    lo = (r - mid.astype(F32)).astype(BF16)
    return hi, mid, lo


def _mm1(a, b, dims=NN):
    return _mm(a.astype(BF16), b.astype(BF16), dims)


_mm_score = _mm1
_mm_neu_y = _mm1
_mm_neu_sq = _mm1
_mm_intra = _mm1
_mm_state = _mm1


def _mmx(a, e, dims=NN):
    h, l = _split2(a)
    return _mm(h, e, dims) + _mm(l, e, dims)


def _xmm(e, a, dims=NN):
    h, m, l = _split3(a)
    return _mm(e, h, dims) + (_mm(e, m, dims) + _mm(e, l, dims))


def _iota(shape, dim):
    return lax.broadcasted_iota(jnp.int32, shape, dim)


def _sigmoid(x):
    return 1.0 / (1.0 + jnp.exp(-x))


def _softplus(x):
    return jnp.maximum(x, 0.0) + jnp.log1p(jnp.exp(-jnp.abs(x)))


def _head_ones(width):
    r = _iota((width, width), 0) >> HEAD_SHIFT
    c = _iota((width, width), 1) >> HEAD_SHIFT
    return jnp.where(r == c, 1.0, 0.0).astype(BF16)


def _params(sem, vmem=VMEM_LIMIT):
    return pltpu.CompilerParams(dimension_semantics=sem, vmem_limit_bytes=vmem)


def _rope_table_kernel(pos_ref, freq_ref, c_ref, s1_ref, s2_ref):
    ang = pos_ref[...].astype(F32) * freq_ref[...]
    c = jnp.cos(ang)
    s = jnp.sin(ang)
    j = _iota(ang.shape, 1) & (ROPE_DIM - 1)
    half = ROPE_DIM // 2
    c_ref[...] = c
    s1_ref[...] = jnp.where(j < half, -s, 0.0)
    s2_ref[...] = jnp.where(j >= half, s, 0.0)


def _rope_tables(positions):
    T = positions.size
    half = ROPE_DIM // 2
    per_row = LANES // ROPE_DIM
    rows = T // per_row
    inv_freq = ROPE_THETA ** (-jnp.arange(half, dtype=F32) / half)
    freq = inv_freq[jnp.arange(LANES) % half][None, :]
    pos_c = jnp.repeat(positions.reshape(T), ROPE_DIM).reshape(rows, LANES)
    tm = min(rows, 1024)
    spec = pl.BlockSpec((tm, LANES), lambda i: (i, 0))
    compact = pl.pallas_call(
        _rope_table_kernel,
        grid=(rows // tm,),
        in_specs=[spec, pl.BlockSpec((1, LANES), lambda i: (0, 0))],
        out_specs=[spec, spec, spec],
        out_shape=[jax.ShapeDtypeStruct((rows, LANES), F32)] * 3,
        compiler_params=_params(("parallel",)),
        name="rope_tables",
    )(pos_c, freq)

    def expand(t, fill):
        t = t.reshape(T, ROPE_DIM)
        head = jnp.concatenate([t, jnp.full((T, HEAD_DIM - ROPE_DIM), fill, F32)], axis=1)
        return jnp.tile(head, (1, LANES // HEAD_DIM))

    return [expand(compact[0], 1.0), expand(compact[1], 0.0), expand(compact[2], 0.0)]


ATT_OUT = 7 * LANES


def _in_proj_kernel(x_ref, gain_ref, watt_ref, wdqkv_ref, wdz_ref, wab_ref, wrw_ref,
                    c_ref, s1_ref, s2_ref,
                    att_ref, dqkv_ref, dz_ref, dab_ref, rwp_ref):
    x = x_ref[...]
    ms = jnp.mean(x * x, axis=-1, keepdims=True)
    h = (x * lax.rsqrt(ms + NORM_EPS) * gain_ref[...]).astype(BF16)
    att = _mm(h, watt_ref[...])
    c, s1, s2 = c_ref[...], s1_ref[...], s2_ref[...]
    half = ROPE_DIM // 2
    for slab in range(4):
        t = att[:, slab * LANES:(slab + 1) * LANES]
        t = t * c + pltpu.roll(t, LANES - half, 1) * s1 + pltpu.roll(t, half, 1) * s2
        if slab < 3:
            att_ref[:, slab * LANES:(slab + 1) * LANES] = (t * (HEAD_DIM ** -0.5)).astype(BF16)
        else:
            att_ref[:, 3 * LANES:4 * LANES] = t.astype(BF16)
            att_ref[:, 5 * LANES:6 * LANES] = pltpu.roll(t, HEAD_DIM, 1).astype(BF16)
    v = att[:, 4 * LANES:5 * LANES]
    att_ref[:, 4 * LANES:5 * LANES] = v.astype(BF16)
    att_ref[:, 6 * LANES:7 * LANES] = pltpu.roll(v, HEAD_DIM, 1).astype(BF16)
    dqkv_ref[...] = _mm(h, wdqkv_ref[...])
    dz_ref[...] = _mm(h, wdz_ref[...])
    dab_ref[...] = _mm(h, wab_ref[...])
    rwp_ref[...] = _mm(h, wrw_ref[...])


def _in_proj(x2, gain, w, tables, tm=512):
    T = x2.shape[0]
    tm = min(tm, T)
    row = lambda wdt: pl.BlockSpec((tm, wdt), lambda i: (i, 0))
    full = lambda a: pl.BlockSpec(a.shape, lambda i: (0,) * a.ndim)
    widths = (ATT_OUT, 3 * DN_WIDTH, DN_WIDTH, LANES, RW_IN)
    dts = (BF16, F32, F32, F32, F32)
    return pl.pallas_call(
        _in_proj_kernel,
        grid=(T // tm,),
        in_specs=[row(D_MODEL), full(gain)] + [full(a) for a in w] + [row(LANES)] * 3,
        out_specs=[row(wd) for wd in widths],
        out_shape=[jax.ShapeDtypeStruct((T, wd), dt) for wd, dt in zip(widths, dts)],
        compiler_params=_params(("parallel",)),
        name="in_proj",
    )(x2, gain, *w, *tables)


def _attn_kernel(sink_ref, q_ref, *refs, tq, seq):
    kv_refs = refs[:12]
    o_ref = refs[12]
    win = refs[13:17]
    i = pl.program_id(1)
    for a in range(4):
        prev, main, nxt = kv_refs[3 * a:3 * a + 3]
        win[a][0:WINDOW, :] = prev[0]
        win[a][WINDOW:WINDOW + tq, :] = main[0]
        win[a][WINDOW + tq:, :] = nxt[0]
    kA, vA, kB, vB = win
    blk = WINDOW
    lane_hi = _iota((3 * blk, LANES), 1) >= HEAD_DIM
    qlane_hi = _iota((blk, LANES), 1) >= HEAD_DIM
    qi = _iota((blk, 3 * blk), 0)
    kk = _iota((blk, 3 * blk), 1)
    rel = kk - blk - qi
    band = (rel <= WINDOW) & (rel >= -WINDOW)
    for s in range(tq // blk):
        r0 = s * blk
        abs_k = i * tq + r0 + kk - blk
        ok = band & (abs_k >= 0) & (abs_k < seq)
        kwin = {0: kA[r0:r0 + 3 * blk, :], 1: kB[r0:r0 + 3 * blk, :]}
        vwin = {0: vA[r0:r0 + 3 * blk, :], 1: vB[r0:r0 + 3 * blk, :]}
        src = [0 if (h // 3) == (h % 2) else 1 for h in range(ATT_HEADS)]
        scores = []
        for h in range(ATT_HEADS):
            qp = q_ref[0, r0:r0 + blk, (h // 2) * LANES:(h // 2 + 1) * LANES]
            is_half = qlane_hi if h % 2 == 1 else jnp.logical_not(qlane_hi)
            qm = jnp.where(is_half, qp, jnp.zeros_like(qp))
            scores.append(_mm(qm, kwin[src[h]], NT))
        probs, inv = [], []
        for h in range(ATT_HEADS):
            sc = jnp.where(ok, scores[h], NEG_BIG)
            sk = sink_ref[h]
            m = jnp.maximum(jnp.max(sc, axis=-1, keepdims=True), sk)
            pexp = jnp.exp(sc - m)
            den = jnp.sum(pexp, axis=-1, keepdims=True) + jnp.exp(sk - m)
            probs.append(pexp.astype(BF16))
            inv.append(1.0 / den)
        for p in range(ATT_HEADS // 2):
            acc = jnp.zeros((blk, LANES), F32)
            for half in range(2):
                h = 2 * p + half
                v_half = lane_hi if half == 1 else jnp.logical_not(lane_hi)
                vm = jnp.where(v_half, vwin[src[h]], jnp.zeros_like(vwin[src[h]]))
                acc = acc + _mm(probs[h], vm) * inv[h]
            o_ref[0, r0:r0 + blk, p * LANES:(p + 1) * LANES] = acc.astype(o_ref.dtype)


def _attention(att3, sink, tq=512):
    B, S, _ = att3.shape
    tq = min(tq, S)
    nb = tq // WINDOW
    last = S // WINDOW - 1
    specs = [pl.BlockSpec(memory_space=pltpu.SMEM),
             pl.BlockSpec((1, tq, ATT_WIDTH), lambda b, i: (b, i, 0))]
    for col in (3, 4, 5, 6):
        specs.append(pl.BlockSpec((1, WINDOW, LANES),
                                  lambda b, i, col=col: (b, jnp.maximum(i * nb - 1, 0), col)))
        specs.append(pl.BlockSpec((1, tq, LANES), lambda b, i, col=col: (b, i, col)))
        specs.append(pl.BlockSpec((1, WINDOW, LANES),
                                  lambda b, i, col=col: (b, jnp.minimum((i + 1) * nb, last), col)))
    return pl.pallas_call(
        functools.partial(_attn_kernel, tq=tq, seq=S),
        grid=(B, S // tq),
        in_specs=specs,
        out_specs=pl.BlockSpec((1, tq, ATT_WIDTH), lambda b, i: (b, i, 0)),
        out_shape=jax.ShapeDtypeStruct((B, S, ATT_WIDTH), BF16),
        scratch_shapes=[pltpu.VMEM((tq + 2 * WINDOW, LANES), BF16)] * 4,
        compiler_params=_params(("parallel", "parallel")),
        name="window_attention",
    )(sink, att3, *([att3] * 12))


def _cumsum_mats():
    rc = _iota((CHUNK, CHUNK), 0)
    cc = _iota((CHUNK, CHUNK), 1)
    return (jnp.where(cc <= rc, 1.0, 0.0).astype(BF16), jnp.where(cc >= rc, 1.0, 0.0).astype(BF16))


def _chunk_consts(reverse):
    C = CHUNK
    ri = _iota((C, LANES), 0)
    li = _iota((C, LANES), 1)
    j = li & (HEAD_DIM - 1)
    r2 = _iota((LANES, LANES), 0)
    c2 = _iota((LANES, LANES), 1)
    if reverse:
        incl, strict = j >= ri, j > ri
    else:
        incl, strict = j <= ri, j < ri
    return dict(
        incl=incl, strict=strict, hi=li >= HEAD_DIM,
        bdmask=(r2 >= HEAD_DIM) == (c2 >= HEAD_DIM),
        tot_row=0 if reverse else C - 1,
    )


def _bd(y):
    hi = (_iota(y.shape, 1) & (LANES - 1)) >= HEAD_DIM
    zero = jnp.zeros_like(y)
    return jnp.concatenate([jnp.where(hi, zero, y), jnp.where(hi, y, zero)], axis=0)


def _apply(x2, y, mm):
    return mm(x2, _bd(y))


def _xsel(a, e):
    h, m, l = _split3(a)
    return _mm(h, e) + (_mm(m, e) + _mm(l, e))


def _gate_select(base):
    r = _iota((LANES, 2 * LANES), 0)
    c = _iota((LANES, 2 * LANES), 1)
    src = base + ((c >> HEAD_SHIFT) & 1) + jnp.where(c >= LANES, 2 * DN_HEADS, 0)
    return jnp.where(r == src, 1.0, 0.0).astype(BF16)


_NEED_STATE = object()


def _neumann(n2, y):
    eye = (_iota(n2.shape, 1) & (HEAD_DIM - 1)) == _iota(n2.shape, 0)
    t = n2 + jnp.where(eye, 1.0, 0.0)
    pw = _apply(n2, n2, _mm_neu_sq)
    yield
    levels = CHUNK.bit_length() - 2
    for _ in range(levels):
        both = _apply(pw, jnp.concatenate([t, pw], axis=1), _mm_neu_sq)
        t = t + both[:, :LANES]
        pw = both[:, LANES:]
        yield
    return _apply(t, y, _mm_neu_y)


def _interleave(gens, first_state, pred):
    n = len(gens)
    results = [None] * n
    waiting = [False] * n
    done = [False] * n
    while not all(done):
        for i in range(n):
            if done[i]:
                continue
            try:
                if waiting[i]:
                    if pred[i] is None:
                        state = first_state[i]
                    elif done[pred[i]]:
                        state = results[pred[i]][1]
                    else:
                        continue
                    waiting[i] = False
                    token = gens[i].send(state)
                else:
                    token = next(gens[i])
                if token is _NEED_STATE:
                    waiting[i] = True
            except StopIteration as stop:
                results[i] = stop.value
                done[i] = True
    return results


def _dn_chunk(q2, k2, v2, gates, base, gc_row, cs):
    C = CHUNK
    gx = _xsel(gates, _gate_select(base))
    yield
    gcb, bb = gx[:, :LANES], gx[:, LANES:]
    kb = k2 * bb
    vb = v2 * bb
    s2 = _mm_score(jnp.concatenate([kb, q2], axis=0), _bd(k2), NT)
    row = cs["tot_row"]
    gtot = jnp.broadcast_to(gcb[row:row + 1, :], (C, LANES))
    eg = jnp.exp(gcb)
    kbe = kb * eg
    qg = q2 * eg
    kg = k2 * jnp.exp(gtot - gcb)
    rj = jnp.broadcast_to(gc_row, (C, LANES))
    yield
    dec = jnp.exp(jnp.where(cs["incl"], gcb - rj, NEG_BIG))
    l2 = jnp.where(cs["strict"], s2[:C] * dec, 0.0)
    qk = s2[C:] * dec
    y = yield from _neumann(-l2, jnp.concatenate([vb, kbe], axis=1))
    value, kcum = y[:, :LANES], y[:, LANES:]
    return value, kcum, qg, kg, qk, gtot


def _dn_state(load, cs):
    C = CHUNK
    P = yield _NEED_STATE
    sp = _mm_state(jnp.concatenate([load(1), load(2)], axis=0), P)
    yield
    vnew = load(0) - sp[:C]
    o = sp[C:] + _apply(load(4), vnew, _mm_state)
    upd = _mm_state(load(3), vnew, TN)
    yield
    gtot = load(5)
    glast = jnp.exp(jnp.concatenate([gtot, gtot], axis=0))
    pn = P * glast + jnp.where(cs["bdmask"], upd, 0.0)
    return o, pn


def _rw_chunk(r2, v2, at, gcb, kd2, b2, cs):
    C = CHUNK
    row = cs["tot_row"]
    gtot = jnp.broadcast_to(gcb[row:row + 1, :], (C, LANES))
    eneg = jnp.exp(-gcb)
    bt = b2 * eneg
    kt = kd2 * eneg
    rt = r2 * jnp.exp(gcb)
    er = jnp.exp(gtot - gcb)
    bh = b2 * er
    kh = kd2 * er
    bk = jnp.concatenate([_bd(bt), _bd(kt)], axis=0)
    sa = _mm_score(at, bk, NT)
    sr = _mm_score(rt, bk, NT)
    yield
    aab = jnp.where(cs["strict"], sa[:, :LANES], 0.0)
    aak = jnp.where(cs["strict"], sa[:, LANES:], 0.0)
    arb = jnp.where(cs["incl"], sr[:, :LANES], 0.0)
    ark = jnp.where(cs["incl"], sr[:, LANES:], 0.0)
    av = _apply(aak, v2, _mm_intra)
    arkv = _apply(ark, v2, _mm_intra)
    yield
    y = yield from _neumann(aab, jnp.concatenate([av, at], axis=1))
    vp, ap = y[:, :LANES], y[:, LANES:]
    return vp, ap, rt, arb, arkv, bh, kh, v2, gtot


def _rw_state(load, cs):
    C = CHUNK
    P = yield _NEED_STATE
    sp = _mm_state(jnp.concatenate([load(1), load(2)], axis=0), P)
    yield
    u = sp[:C] + load(0)
    v2 = load(7)
    out = sp[C:] + _apply(load(3), u, _mm_state) + load(4)
    upd = _mm_state(jnp.concatenate([load(5), load(6)], axis=0), jnp.concatenate([u, v2], axis=0), TN)
    yield
    gtot = load(8)
    wtot = jnp.exp(jnp.concatenate([gtot, gtot], axis=0))
    pn = P * wtot.T + jnp.where(cs["bdmask"], upd, 0.0)
    return out, pn


def _dn_prep_kernel(prev_ref, main_ref, next_ref, dab_ref, cw_ref, alog_ref, dt_ref,
                    qkv_ref, gates_ref, xbuf, *, tm):
    i = pl.program_id(1)
    n = pl.num_programs(1)
    halo = 8
    pad = (DN_CONV - 1) // 2
    xbuf[0:halo, :] = jnp.where(i == 0, 0.0, prev_ref[0])
    xbuf[halo:halo + tm, :] = main_ref[0]
    xbuf[halo + tm:, :] = jnp.where(i == n - 1, 0.0, next_ref[0])
    ones = _head_ones(LANES)
    for slab in range(3 * DN_WIDTH // LANES):
        cols = slice(slab * LANES, (slab + 1) * LANES)
        y = jnp.zeros((tm, LANES), F32)
        for j in range(DN_CONV):
            y = y + cw_ref[j:j + 1, cols] * xbuf[halo - pad + j:halo - pad + j + tm, cols]
        y = y * _sigmoid(y)
        if slab < 6:
            ss = _mmx(y * y, ones)
            y = y * lax.rsqrt(ss + 1e-6)
        if slab < 3:
            y = y * (HEAD_DIM ** -0.5)
        qkv_ref[0, :, cols] = y
    ab = dab_ref[0]
    lane = _iota((CHUNK, LANES), 1)
    g = -jnp.exp(alog_ref[...]) * _softplus(ab + dt_ref[...])
    beta = _sigmoid(ab)
    tm_f, tm_b = _cumsum_mats()
    for c in range(tm // CHUNK):
        rows = slice(c * CHUNK, (c + 1) * CHUNK)
        gc = jnp.where(lane < DN_HEADS, _xmm(tm_f, g[rows]), _xmm(tm_b, g[rows]))
        gates_ref[0, rows, :] = jnp.where(lane < 2 * DN_HEADS, gc, beta[rows])


def _dn_prep(dqkv3, dab3, conv_w, a_log, dt_bias, tm=512):
    B, S, W = dqkv3.shape
    tm = min(tm, S)
    nb8 = tm // 8
    last8 = S // 8 - 1
    cw = jnp.pad(conv_w, ((0, 8 - DN_CONV), (0, 0)))
    pad_row = lambda a: jnp.pad(a.reshape(1, -1), ((0, 0), (0, LANES - a.size)))
    small = lambda a: pl.BlockSpec(a.shape, lambda b, i: (0,) * a.ndim)
    alog, dt = pad_row(a_log), pad_row(dt_bias)
    return pl.pallas_call(
        functools.partial(_dn_prep_kernel, tm=tm),
        grid=(B, S // tm),
        in_specs=[pl.BlockSpec((1, 8, W), lambda b, i: (b, jnp.maximum(i * nb8 - 1, 0), 0)),
                  pl.BlockSpec((1, tm, W), lambda b, i: (b, i, 0)),
                  pl.BlockSpec((1, 8, W), lambda b, i: (b, jnp.minimum((i + 1) * nb8, last8), 0)),
                  pl.BlockSpec((1, tm, LANES), lambda b, i: (b, i, 0)),
                  small(cw), small(alog), small(dt)],
        out_specs=[pl.BlockSpec((1, tm, W), lambda b, i: (b, i, 0)),
                   pl.BlockSpec((1, tm, LANES), lambda b, i: (b, i, 0))],
        out_shape=[jax.ShapeDtypeStruct((B, S, W), F32),
                   jax.ShapeDtypeStruct((B, S, LANES), F32)],
        scratch_shapes=[pltpu.VMEM((tm + 16, W), F32)],
        compiler_params=_params(("parallel", "parallel")),
        name="deltanet_prep",
    )(dqkv3, dqkv3, dqkv3, dab3, cw, alog, dt)


def _scan_step(state, insts, second):
    def whole(gen, cs, write):
        mid = yield from gen
        o, pn = yield from second(lambda a: mid[a], cs)
        write(o)
        return None, pn

    gens, pred, first = [], [], []
    last = {}
    for i, (chain, gen, cs, write) in enumerate(insts):
        gens.append(whole(gen, cs, write))
        pred.append(last.get(chain))
        first.append(None if chain in last else state[chain])
        last[chain] = i
    results = _interleave(gens, first, pred)
    for chain, i in last.items():
        state[chain] = results[i][1]


def _scan_reset(state):
    @pl.when(pl.program_id(1) == 0)
    def _():
        state[...] = jnp.zeros_like(state)


def _dn_scan_kernel(qf_ref, gf_ref, tf_ref, qb_ref, gb_ref, tb_ref, of_ref, ob_ref, state, *, cpb):
    _scan_reset(state)
    pairs = DN_HEADS // 2
    consts = [_chunk_consts(reverse=False), _chunk_consts(reverse=True)]
    insts = []
    for step in range(cpb):
        for d, (q_ref, g_ref, t_ref, o_ref) in enumerate(((qf_ref, gf_ref, tf_ref, of_ref),
                                                          (qb_ref, gb_ref, tb_ref, ob_ref))):
            c = step if d == 0 else cpb - 1 - step
            rows = slice(c * CHUNK, (c + 1) * CHUNK)
            gates = g_ref[0, rows, :]
            gc_rows = t_ref[0, c]
            for p in range(pairs):
                cols = slice(p * LANES, (p + 1) * LANES)
                q2 = q_ref[0, rows, p * LANES:(p + 1) * LANES]
                k2 = q_ref[0, rows, DN_WIDTH + p * LANES:DN_WIDTH + (p + 1) * LANES]
                v2 = q_ref[0, rows, 2 * DN_WIDTH + p * LANES:2 * DN_WIDTH + (p + 1) * LANES]
                gc_row = gc_rows[d * pairs + p:d * pairs + p + 1, :]
                gen = _dn_chunk(q2, k2, v2, gates, d * DN_HEADS + 2 * p, gc_row, consts[d])

                def write(o, o_ref=o_ref, rows=rows, cols=cols):
                    o_ref[0, rows, cols] = o

                insts.append(((d, p), gen, consts[d], write))
    _scan_step(state, insts, _dn_state)


def _scan_maps(nblk, ndim):
    tail = (0,) * (ndim - 2)
    return (lambda b, n: (b, n) + tail, lambda b, n: (b, nblk - 1 - n) + tail)


def _dn_scan(qkv3, gates3, cpb=8):
    B, S, W = qkv3.shape
    blk = cpb * CHUNK
    nblk = S // blk
    nchunk = S // CHUNK
    pairs = DN_HEADS // 2
    gct = gates3[..., :2 * DN_HEADS].reshape(B, nchunk, CHUNK, 2, pairs, 2)
    gct = gct.transpose(0, 1, 3, 4, 5, 2).reshape(B, nchunk, 2 * pairs, LANES)
    gct = jnp.pad(gct, ((0, 0), (0, 0), (0, 8 - 2 * pairs), (0, 0)))
    fwd, bwd = _scan_maps(nblk, 3)
    fwd4, bwd4 = _scan_maps(nblk, 4)
    return pl.pallas_call(
        functools.partial(_dn_scan_kernel, cpb=cpb),
        grid=(B, nblk),
        in_specs=[pl.BlockSpec((1, blk, W), fwd), pl.BlockSpec((1, blk, LANES), fwd),
                  pl.BlockSpec((1, cpb, 8, LANES), fwd4),
                  pl.BlockSpec((1, blk, W), bwd), pl.BlockSpec((1, blk, LANES), bwd),
                  pl.BlockSpec((1, cpb, 8, LANES), bwd4)],
        out_specs=[pl.BlockSpec((1, blk, DN_WIDTH), fwd), pl.BlockSpec((1, blk, DN_WIDTH), bwd)],
        out_shape=[jax.ShapeDtypeStruct((B, S, DN_WIDTH), F32)] * 2,
        scratch_shapes=[pltpu.VMEM((2, DN_HEADS // 2, LANES, LANES), F32)],
        compiler_params=_params(("parallel", "arbitrary")),
        name="deltanet_scan",
    )(qkv3, gates3, gct, qkv3, gates3, gct)


RW_OUTS = 12


def _rw_prep_kernel(prev_ref, main_ref, next_ref, mu_ref, w0_ref, a0_ref, kk_ref, ka_ref, rk_ref,
                    wup_ref, aup_ref, gup_ref, *refs, tm):
    outs = refs[:RW_OUTS]
    xbuf, pbuf = refs[RW_OUTS:]
    i = pl.program_id(1)
    n = pl.num_programs(1)
    halo = 8
    xbuf[0:halo, :] = jnp.where(i == 0, 0.0, prev_ref[0])
    xbuf[halo:halo + tm, :] = main_ref[0]
    xbuf[halo + tm:, :] = jnp.where(i == n - 1, 0.0, next_ref[0])
    for slab in range(RW_IN // LANES):
        cols = slice(slab * LANES, (slab + 1) * LANES)
        cur = xbuf[halo:halo + tm, cols]
        prv = xbuf[halo - 1:halo - 1 + tm, cols]
        nxt = xbuf[halo + 1:halo + 1 + tm, cols]
        pbuf[:, cols] = cur + mu_ref[0:1, cols] * (prv - cur) + mu_ref[1:2, cols] * (nxt - cur)
    r = pbuf[:, 0:RW_WIDTH]
    k = pbuf[:, RW_WIDTH:2 * RW_WIDTH]
    v = pbuf[:, 2 * RW_WIDTH:3 * RW_WIDTH]
    low = pbuf[:, 3 * RW_WIDTH:3 * RW_WIDTH + LANES]
    gd = pbuf[:, 3 * RW_WIDTH + LANES:]
    ones = _head_ones(RW_WIDTH)
    r_o, v_o, g_o, bg_o = outs[:4]
    g = _mm1(_sigmoid(gd), gup_ref[...])
    kq = k * kk_ref[...]
    kk = kq * lax.rsqrt(_mmx(kq * kq, ones) + 1e-6)
    bonus = _mmx(r * k * rk_ref[...], ones) * v
    r_o[0] = r
    v_o[0] = v
    g_o[0] = g
    bg_o[0] = bonus * g
    wl = jnp.tanh(low)
    cum = _cumsum_mats()
    for d in range(2):
        at_o, gc_o, kd_o, b_o = outs[4 + 4 * d:8 + 4 * d]
        w = -_softplus(-(w0_ref[d:d + 1, :] + _mm1(wl, wup_ref[d]))) - 0.5
        lw = -jnp.exp(w)
        for c in range(tm // CHUNK):
            rows = slice(c * CHUNK, (c + 1) * CHUNK)
            gc = _xmm(cum[d], lw[rows])
            gc_o[0, rows, :] = gc
            at_o[0, rows, :] = -kk[rows] * jnp.exp(gc - lw[rows])
        a = _sigmoid(a0_ref[d:d + 1, :] + _mm1(low, aup_ref[d]))
        kd_o[0] = k * (1.0 + (a - 1.0) * ka_ref[...])
        b_o[0] = kk * a


def _rw_prep(rwp3, mu, w0, w_up, a0, a_up, g_up, k_k, k_a, r_k, tm=512):
    B, S, W = rwp3.shape
    tm = min(tm, S)
    nb8 = tm // 8
    last8 = S // 8 - 1
    zeros = jnp.zeros((2, W_RANK, RW_WIDTH), F32)
    wup = jnp.concatenate([w_up, zeros], axis=1)
    aup = jnp.concatenate([zeros, a_up], axis=1)
    rowv = lambda a: a.reshape(1, RW_WIDTH)
    small = lambda a: pl.BlockSpec(a.shape, lambda b, i: (0,) * a.ndim)
    params = [mu, w0, a0, rowv(k_k), rowv(k_a), rowv(r_k), wup, aup, g_up]
    tile = pl.BlockSpec((1, tm, RW_WIDTH), lambda b, i: (b, i, 0))
    return pl.pallas_call(
        functools.partial(_rw_prep_kernel, tm=tm),
        grid=(B, S // tm),
        in_specs=[pl.BlockSpec((1, 8, W), lambda b, i: (b, jnp.maximum(i * nb8 - 1, 0), 0)),
                  pl.BlockSpec((1, tm, W), lambda b, i: (b, i, 0)),
                  pl.BlockSpec((1, 8, W), lambda b, i: (b, jnp.minimum((i + 1) * nb8, last8), 0))]
                 + [small(a) for a in params],
        out_specs=[tile] * RW_OUTS,
        out_shape=[jax.ShapeDtypeStruct((B, S, RW_WIDTH), F32)] * RW_OUTS,
        scratch_shapes=[pltpu.VMEM((tm + 16, W), F32), pltpu.VMEM((tm, W), F32)],
        compiler_params=_params(("parallel", "parallel")),
        name="rwkv_prep",
    )(rwp3, rwp3, rwp3, *params)


def _rw_scan_kernel(*refs, cpb):
    ins = refs[:12]
    yf_ref, yb_ref, state = refs[12:]
    _scan_reset(state)
    pairs = RW_HEADS // 2
    consts = [_chunk_consts(reverse=False), _chunk_consts(reverse=True)]
    insts = []
    for step in range(cpb):
        for d, o_ref in enumerate((yf_ref, yb_ref)):
            src = ins[6 * d:6 * d + 6]
            c = step if d == 0 else cpb - 1 - step
            rows = slice(c * CHUNK, (c + 1) * CHUNK)
            for p in range(pairs):
                cols = slice(p * LANES, (p + 1) * LANES)
                gen = _rw_chunk(*[ref[0, rows, cols] for ref in src], consts[d])

                def write(o, o_ref=o_ref, rows=rows, cols=cols):
                    o_ref[0, rows, cols] = o

                insts.append(((d, p), gen, consts[d], write))
    _scan_step(state, insts, _rw_state)


def _rw_scan(r, v, dirs, cpb=8):
    B, S, W = r.shape
    blk = cpb * CHUNK
    nblk = S // blk
    fwd, bwd = [pl.BlockSpec((1, blk, W), m) for m in _scan_maps(nblk, 3)]
    return pl.pallas_call(
        functools.partial(_rw_scan_kernel, cpb=cpb),
        grid=(B, nblk),
        in_specs=[fwd] * 6 + [bwd] * 6,
        out_specs=[fwd, bwd],
        out_shape=[jax.ShapeDtypeStruct((B, S, W), F32)] * 2,
        scratch_shapes=[pltpu.VMEM((2, RW_HEADS // 2, LANES, LANES), F32)],
        compiler_params=_params(("parallel", "arbitrary")),
        name="rwkv_scan",
    )(r, v, *dirs[0], r, v, *dirs[1])


def _out_proj_kernel(x_ref, ya_ref, of_ref, ob_ref, z_ref, dnw_ref, yf_ref, yb_ref, g_ref, bg_ref,
                     lnw_ref, lnb_ref, wa_ref, wd_ref, wr_ref, gain_ref, xo_ref, h_ref):
    o = of_ref[...] + ob_ref[...]
    ms = _mmx(o * o, _head_ones(DN_WIDTH)) * (1.0 / HEAD_DIM)
    z = z_ref[...]
    ydn = o * lax.rsqrt(ms + NORM_EPS) * dnw_ref[...] * (z * _sigmoid(z))
    y = yf_ref[...] + yb_ref[...]
    ones = _head_ones(RW_WIDTH)
    mean = _mmx(y, ones) * (1.0 / HEAD_DIM)
    yc = y - mean
    var = _mmx(yc * yc, ones) * (1.0 / HEAD_DIM)
    yrw = (yc * lax.rsqrt(var + RW_LN_EPS) * lnw_ref[...] + lnb_ref[...]) * g_ref[...] + bg_ref[...]
    acc = _mm(ya_ref[...], wa_ref[...])
    acc = acc + _mm(ydn.astype(BF16), wd_ref[...])
    acc = acc + _mm(yrw.astype(BF16), wr_ref[...])
    xn = x_ref[...] + acc
    xo_ref[...] = xn
    ms2 = jnp.mean(xn * xn, axis=-1, keepdims=True)
    h_ref[...] = (xn * lax.rsqrt(ms2 + NORM_EPS) * gain_ref[...]).astype(BF16)


def _out_proj(x2, ya, of, ob, z, dnw, yf, yb, g, bg, lnw, lnb, w_out, gain, tm=512):
    T = x2.shape[0]
    tm = min(tm, T)
    row = lambda a: pl.BlockSpec((tm, a.shape[1]), lambda i: (i, 0))
    full = lambda a: pl.BlockSpec(a.shape, lambda i: (0,) * a.ndim)
    wa = w_out[:ATT_WIDTH].astype(BF16)
    wd = w_out[ATT_WIDTH:ATT_WIDTH + DN_WIDTH].astype(BF16)
    wr = w_out[ATT_WIDTH + DN_WIDTH:].astype(BF16)
    args = [x2, ya, of, ob, z, dnw, yf, yb, g, bg, lnw, lnb, wa, wd, wr, gain]
    is_row = [True, True, True, True, True, False, True, True, True, True, False, False,
              False, False, False, False]
    return pl.pallas_call(
        _out_proj_kernel,
        grid=(T // tm,),
        in_specs=[row(a) if r else full(a) for a, r in zip(args, is_row)],
        out_specs=[pl.BlockSpec((tm, D_MODEL), lambda i: (i, 0))] * 2,
        out_shape=[jax.ShapeDtypeStruct((T, D_MODEL), F32), jax.ShapeDtypeStruct((T, D_MODEL), BF16)],
        compiler_params=_params(("parallel",)),
        name="out_proj",
    )(*args)


FFN_HALO = 16


def _ffn_kernel(prev_ref, main_ref, next_ref, x_ref, wg_ref, wv_ref, cg_ref, cv_ref, wd_ref, fin_ref,
                o_ref, *, tm, final_norm):
    i = pl.program_id(1)
    ni = pl.num_programs(1)
    zero = jnp.zeros((FFN_HALO, D_MODEL), BF16)
    h = jnp.concatenate([jnp.where(i == 0, zero, prev_ref[0]), main_ref[0],
                         jnp.where(i == ni - 1, zero, next_ref[0])], axis=0)
    rows = tm + 2 * FFN_HALO

    def conv(u, c_ref):
        um = pltpu.roll(u, 1, 0)
        up = pltpu.roll(u, rows - 1, 0)
        y = c_ref[0:1, :] * um + c_ref[1:2, :] * u + c_ref[2:3, :] * up
        return y[FFN_HALO:FFN_HALO + tm]

    gate = conv(_mm(h, wg_ref[...]), cg_ref)
    val = conv(_mm(h, wv_ref[...]), cv_ref)
    act = (gate * _sigmoid(gate) * val).astype(BF16)
    xn = x_ref[0] + _mm(act, wd_ref[...])
    if final_norm:
        ms = jnp.mean(xn * xn, axis=-1, keepdims=True)
        xn = xn * lax.rsqrt(ms + NORM_EPS) * fin_ref[...]
    o_ref[0] = xn


def _ffn(h3, x3, w_up, conv_w, w_down, fin_gain, final_norm, tm=512):
    B, S, _ = h3.shape
    tm = min(tm, S)
    nbh = tm // FFN_HALO
    lasth = S // FFN_HALO - 1
    wup = w_up.astype(BF16)
    wdn = w_down.astype(BF16)
    cw = jnp.pad(conv_w, ((0, 8 - conv_w.shape[0]), (0, 0)))
    once = pl.Buffered(1)
    return pl.pallas_call(
        functools.partial(_ffn_kernel, tm=tm, final_norm=final_norm),
        grid=(B, S // tm),
        in_specs=[
            pl.BlockSpec((1, FFN_HALO, D_MODEL), lambda b, i: (b, jnp.maximum(i * nbh - 1, 0), 0)),
            pl.BlockSpec((1, tm, D_MODEL), lambda b, i: (b, i, 0)),
            pl.BlockSpec((1, FFN_HALO, D_MODEL), lambda b, i: (b, jnp.minimum((i + 1) * nbh, lasth), 0)),
            pl.BlockSpec((1, tm, D_MODEL), lambda b, i: (b, i, 0)),
            pl.BlockSpec((D_MODEL, D_FF), lambda b, i: (0, 0), pipeline_mode=once),
            pl.BlockSpec((D_MODEL, D_FF), lambda b, i: (0, 1), pipeline_mode=once),
            pl.BlockSpec((8, D_FF), lambda b, i: (0, 0), pipeline_mode=once),
            pl.BlockSpec((8, D_FF), lambda b, i: (0, 1), pipeline_mode=once),
            pl.BlockSpec((D_FF, D_MODEL), lambda b, i: (0, 0), pipeline_mode=once),
            pl.BlockSpec((1, D_MODEL), lambda b, i: (0, 0)),
        ],
        out_specs=pl.BlockSpec((1, tm, D_MODEL), lambda b, i: (b, i, 0)),
        out_shape=jax.ShapeDtypeStruct((B, S, D_MODEL), F32),
        compiler_params=_params(("parallel", "parallel")),
        name="conv_ffn",
    )(h3, h3, h3, x3, wup, wup, cw, cw, wdn, fin_gain)


def _in_weights(w_in_l):
    a = ATT_WIDTH + 2 * ATT_KV_WIDTH
    d = a + 3 * DN_WIDTH
    z = d + DN_WIDTH
    g = z + 4 * DN_HEADS
    w_ab = jnp.pad(w_in_l[:, z:g], ((0, 0), (0, LANES - 4 * DN_HEADS)))
    groups = (w_in_l[:, :a], w_in_l[:, a:d], w_in_l[:, d:z], w_ab, w_in_l[:, g:])
    return [w.astype(BF16) for w in groups]


def kernel(x, positions, norm_mix, w_in, attn_sink, dn_conv, dn_a_log, dn_dt_bias, dn_norm, rw_mu, rw_w0, rw_w_up, rw_a0, rw_a_up, rw_g_up, rw_k_k, rw_k_a, rw_r_k, rw_ln_w, rw_ln_b, w_out, norm_ffn, ffn_w_up, ffn_conv, ffn_w_down, norm_final):
    B, S, D = x.shape
    T = B * S
    depth = w_in.shape[0]
    tables = _rope_tables(positions)
    x2 = x.reshape(T, D)
    for l in range(depth):
        att, dqkv, dz, dab, rwp = _in_proj(x2, norm_mix[l].reshape(1, D), _in_weights(w_in[l]), tables)
        y_att = _attention(att.reshape(B, S, ATT_OUT), attn_sink[l])
        qkv_act, gates = _dn_prep(dqkv.reshape(B, S, -1), dab.reshape(B, S, LANES),
                                  dn_conv[l], dn_a_log[l], dn_dt_bias[l])
        o_f, o_b = _dn_scan(qkv_act, gates)
        rw = _rw_prep(rwp.reshape(B, S, RW_IN), rw_mu[l], rw_w0[l], rw_w_up[l], rw_a0[l], rw_a_up[l],
                      rw_g_up[l], rw_k_k[l], rw_k_a[l], rw_r_k[l])
        r, v, g, bg = rw[:4]
        y_f, y_b = _rw_scan(r, v, (rw[4:8], rw[8:12]))
        flat = lambda a: a.reshape(T, a.shape[-1])
        x2, h2 = _out_proj(
            x2, flat(y_att), flat(o_f), flat(o_b), dz,
            jnp.tile(dn_norm[l], DN_HEADS).reshape(1, DN_WIDTH),
            flat(y_f), flat(y_b), flat(g), flat(bg),
            rw_ln_w[l].reshape(1, RW_WIDTH), rw_ln_b[l].reshape(1, RW_WIDTH),
            w_out[l], norm_ffn[l].reshape(1, D))
        x2 = _ffn(h2.reshape(B, S, D), x2.reshape(B, S, D), ffn_w_up[l], ffn_conv[l], ffn_w_down[l],
                  norm_final.reshape(1, D), final_norm=(l == depth - 1)).reshape(T, D)
    return x2.reshape(B, S, D)
```

```python
import functools

import jax
import jax.numpy as jnp
from jax import lax
from jax.experimental import pallas as pl
from jax.experimental.pallas import tpu as pltpu

F32 = jnp.float32
BF16 = jnp.bfloat16

D_MODEL = 1024
HEAD_DIM = 64
HEAD_SHIFT = 6
ATT_HEADS = 6
ATT_WIDTH = 384
ATT_KV_WIDTH = 128
WINDOW = 128
ROPE_DIM = 16
ROPE_THETA = 500000.0
DN_HEADS = 6
DN_WIDTH = 384
DN_CONV = 5
RW_HEADS = 4
RW_WIDTH = 256
W_RANK = 64
A_RANK = 64
G_RANK = 128
RW_IN = 1024
D_FF = 2816
NORM_EPS = 1e-6
RW_LN_EPS = 64e-5
LANES = 128
CHUNK = 64
NEG_BIG = -1e30

NN = (((1,), (0,)), ((), ()))
NT = (((1,), (1,)), ((), ()))
TN = (((0,), (0,)), ((), ()))

VMEM_LIMIT = 56 * 1024 * 1024


def _mm(a, b, dims=NN):
    return lax.dot_general(a, b, dims, preferred_element_type=F32)


def _split2(x):
    hi = x.astype(BF16)
    lo = (x - hi.astype(F32)).astype(BF16)
    return hi, lo


def _split3(x):
    hi = x.astype(BF16)
    r = x - hi.astype(F32)
    mid = r.astype(BF16)
    lo = (r - mid.astype(F32)).astype(BF16)
    return hi, mid, lo


def _mm1(a, b, dims=NN):
    return _mm(a.astype(BF16), b.astype(BF16), dims)


_mm_score = _mm1
_mm_neu_y = _mm1
_mm_neu_sq = _mm1
_mm_intra = _mm1
_mm_state = _mm1


def _mmx(a, e, dims=NN):
    h, l = _split2(a)
    return _mm(h, e, dims) + _mm(l, e, dims)


def _xmm(e, a, dims=NN):
    h, m, l = _split3(a)
    return _mm(e, h, dims) + (_mm(e, m, dims) + _mm(e, l, dims))


def _iota(shape, dim):
    return lax.broadcasted_iota(jnp.int32, shape, dim)


def _sigmoid(x):
    return 1.0 / (1.0 + jnp.exp(-x))


def _softplus(x):
    return jnp.maximum(x, 0.0) + jnp.log1p(jnp.exp(-jnp.abs(x)))


def _head_ones(width):
    r = _iota((width, width), 0) >> HEAD_SHIFT
    c = _iota((width, width), 1) >> HEAD_SHIFT
    return jnp.where(r == c, 1.0, 0.0).astype(BF16)


def _params(sem, vmem=VMEM_LIMIT):
    return pltpu.CompilerParams(dimension_semantics=sem, vmem_limit_bytes=vmem)


def _rope_table_kernel(pos_ref, freq_ref, c_ref, s1_ref, s2_ref):
    ang = pos_ref[...].astype(F32) * freq_ref[...]
    c = jnp.cos(ang)
    s = jnp.sin(ang)
    j = _iota(ang.shape, 1) & (ROPE_DIM - 1)
    half = ROPE_DIM // 2
    c_ref[...] = c
    s1_ref[...] = jnp.where(j < half, -s, 0.0)
    s2_ref[...] = jnp.where(j >= half, s, 0.0)


def _rope_tables(positions):
    T = positions.size
    half = ROPE_DIM // 2
    per_row = LANES // ROPE_DIM
    rows = T // per_row
    inv_freq = ROPE_THETA ** (-jnp.arange(half, dtype=F32) / half)
    freq = inv_freq[jnp.arange(LANES) % half][None, :]
    pos_c = jnp.repeat(positions.reshape(T), ROPE_DIM).reshape(rows, LANES)
    tm = min(rows, 1024)
    spec = pl.BlockSpec((tm, LANES), lambda i: (i, 0))
    compact = pl.pallas_call(
        _rope_table_kernel,
        grid=(rows // tm,),
        in_specs=[spec, pl.BlockSpec((1, LANES), lambda i: (0, 0))],
        out_specs=[spec, spec, spec],
        out_shape=[jax.ShapeDtypeStruct((rows, LANES), F32)] * 3,
        compiler_params=_params(("parallel",)),
        name="rope_tables",
    )(pos_c, freq)

    def expand(t, fill):
        t = t.reshape(T, ROPE_DIM)
        head = jnp.concatenate([t, jnp.full((T, HEAD_DIM - ROPE_DIM), fill, F32)], axis=1)
        return jnp.tile(head, (1, LANES // HEAD_DIM))

    return [expand(compact[0], 1.0), expand(compact[1], 0.0), expand(compact[2], 0.0)]


ATT_OUT = 7 * LANES


def _in_proj_kernel(x_ref, gain_ref, watt_ref, wdqkv_ref, wdz_ref, wab_ref, wrw_ref,
                    c_ref, s1_ref, s2_ref,
                    att_ref, dqkv_ref, dz_ref, dab_ref, rwp_ref):
    x = x_ref[...]
    ms = jnp.mean(x * x, axis=-1, keepdims=True)
    h = (x * lax.rsqrt(ms + NORM_EPS) * gain_ref[...]).astype(BF16)
    att = _mm(h, watt_ref[...])
    c, s1, s2 = c_ref[...], s1_ref[...], s2_ref[...]
    half = ROPE_DIM // 2
    for slab in range(4):
        t = att[:, slab * LANES:(slab + 1) * LANES]
        t = t * c + pltpu.roll(t, LANES - half, 1) * s1 + pltpu.roll(t, half, 1) * s2
        if slab < 3:
            att_ref[:, slab * LANES:(slab + 1) * LANES] = (t * (HEAD_DIM ** -0.5)).astype(BF16)
        else:
            att_ref[:, 3 * LANES:4 * LANES] = t.astype(BF16)
            att_ref[:, 5 * LANES:6 * LANES] = pltpu.roll(t, HEAD_DIM, 1).astype(BF16)
    v = att[:, 4 * LANES:5 * LANES]
    att_ref[:, 4 * LANES:5 * LANES] = v.astype(BF16)
    att_ref[:, 6 * LANES:7 * LANES] = pltpu.roll(v, HEAD_DIM, 1).astype(BF16)
    dqkv_ref[...] = _mm(h, wdqkv_ref[...])
    dz_ref[...] = _mm(h, wdz_ref[...])
    dab_ref[...] = _mm(h, wab_ref[...])
    rwp_ref[...] = _mm(h, wrw_ref[...])


def _in_proj(x2, gain, w, tables, tm=1024):
    T = x2.shape[0]
    tm = min(tm, T)
    row = lambda wdt: pl.BlockSpec((tm, wdt), lambda i: (i, 0))
    full = lambda a: pl.BlockSpec(a.shape, lambda i: (0,) * a.ndim, pipeline_mode=pl.Buffered(1))
    widths = (ATT_OUT, 3 * DN_WIDTH, DN_WIDTH, LANES, RW_IN)
    dts = (BF16, F32, F32, F32, F32)
    return pl.pallas_call(
        _in_proj_kernel,
        grid=(T // tm,),
        in_specs=[row(D_MODEL), full(gain)] + [full(a) for a in w] + [row(LANES)] * 3,
        out_specs=[row(wd) for wd in widths],
        out_shape=[jax.ShapeDtypeStruct((T, wd), dt) for wd, dt in zip(widths, dts)],
        compiler_params=_params(("parallel",)),
        name="in_proj",
    )(x2, gain, *w, *tables)


def _attn_kernel(sink_ref, q_ref, *refs, tq, seq):
    kv_refs = refs[:12]
    o_ref = refs[12]
    win = refs[13:17]
    i = pl.program_id(1)
    for a in range(4):
        prev, main, nxt = kv_refs[3 * a:3 * a + 3]
        win[a][0:WINDOW, :] = prev[0]
        win[a][WINDOW:WINDOW + tq, :] = main[0]
        win[a][WINDOW + tq:, :] = nxt[0]
    kA, vA, kB, vB = win
    blk = WINDOW
    lane_hi = _iota((3 * blk, LANES), 1) >= HEAD_DIM
    qlane_hi = _iota((blk, LANES), 1) >= HEAD_DIM
    qi = _iota((blk, 3 * blk), 0)
    kk = _iota((blk, 3 * blk), 1)
    rel = kk - blk - qi
    band = (rel <= WINDOW) & (rel >= -WINDOW)
    for s in range(tq // blk):
        r0 = s * blk
        abs_k = i * tq + r0 + kk - blk
        ok = band & (abs_k >= 0) & (abs_k < seq)
        kwin = {0: kA[r0:r0 + 3 * blk, :], 1: kB[r0:r0 + 3 * blk, :]}
        vwin = {0: vA[r0:r0 + 3 * blk, :], 1: vB[r0:r0 + 3 * blk, :]}
        src = [0 if (h // 3) == (h % 2) else 1 for h in range(ATT_HEADS)]
        scores = []
        for h in range(ATT_HEADS):
            qp = q_ref[0, r0:r0 + blk, (h // 2) * LANES:(h // 2 + 1) * LANES]
            is_half = qlane_hi if h % 2 == 1 else jnp.logical_not(qlane_hi)
            qm = jnp.where(is_half, qp, jnp.zeros_like(qp))
            scores.append(_mm(qm, kwin[src[h]], NT))
        probs, inv = [], []
        for h in range(ATT_HEADS):
            sc = jnp.where(ok, scores[h], NEG_BIG)
            sk = sink_ref[h]
            m = jnp.maximum(jnp.max(sc, axis=-1, keepdims=True), sk)
            pexp = jnp.exp(sc - m)
            den = jnp.sum(pexp, axis=-1, keepdims=True) + jnp.exp(sk - m)
            probs.append(pexp.astype(BF16))
            inv.append(1.0 / den)
        for p in range(ATT_HEADS // 2):
            acc = jnp.zeros((blk, LANES), F32)
            for half in range(2):
                h = 2 * p + half
                v_half = lane_hi if half == 1 else jnp.logical_not(lane_hi)
                vm = jnp.where(v_half, vwin[src[h]], jnp.zeros_like(vwin[src[h]]))
                acc = acc + _mm(probs[h], vm) * inv[h]
            o_ref[0, r0:r0 + blk, p * LANES:(p + 1) * LANES] = acc.astype(o_ref.dtype)


def _attention(att3, sink, tq=512):
    B, S, _ = att3.shape
    tq = min(tq, S)
    nb = tq // WINDOW
    last = S // WINDOW - 1
    specs = [pl.BlockSpec(memory_space=pltpu.SMEM),
             pl.BlockSpec((1, tq, ATT_WIDTH), lambda b, i: (b, i, 0))]
    for col in (3, 4, 5, 6):
        specs.append(pl.BlockSpec((1, WINDOW, LANES),
                                  lambda b, i, col=col: (b, jnp.maximum(i * nb - 1, 0), col)))
        specs.append(pl.BlockSpec((1, tq, LANES), lambda b, i, col=col: (b, i, col)))
        specs.append(pl.BlockSpec((1, WINDOW, LANES),
                                  lambda b, i, col=col: (b, jnp.minimum((i + 1) * nb, last), col)))
    return pl.pallas_call(
        functools.partial(_attn_kernel, tq=tq, seq=S),
        grid=(B, S // tq),
        in_specs=specs,
        out_specs=pl.BlockSpec((1, tq, ATT_WIDTH), lambda b, i: (b, i, 0)),
        out_shape=jax.ShapeDtypeStruct((B, S, ATT_WIDTH), BF16),
        scratch_shapes=[pltpu.VMEM((tq + 2 * WINDOW, LANES), BF16)] * 4,
        compiler_params=_params(("parallel", "parallel")),
        name="window_attention",
    )(sink, att3, *([att3] * 12))


def _cumsum_mats():
    rc = _iota((CHUNK, CHUNK), 0)
    cc = _iota((CHUNK, CHUNK), 1)
    return (jnp.where(cc <= rc, 1.0, 0.0).astype(BF16), jnp.where(cc >= rc, 1.0, 0.0).astype(BF16))


def _chunk_consts(reverse):
    C = CHUNK
    ri = _iota((C, LANES), 0)
    li = _iota((C, LANES), 1)
    j = li & (HEAD_DIM - 1)
    r2 = _iota((LANES, LANES), 0)
    c2 = _iota((LANES, LANES), 1)
    if reverse:
        incl, strict = j >= ri, j > ri
    else:
        incl, strict = j <= ri, j < ri
    return dict(
        incl=incl, strict=strict, hi=li >= HEAD_DIM,
        bdmask=(r2 >= HEAD_DIM) == (c2 >= HEAD_DIM),
        tot_row=0 if reverse else C - 1,
    )


def _bd(y):
    hi = (_iota(y.shape, 1) & (LANES - 1)) >= HEAD_DIM
    zero = jnp.zeros_like(y)
    return jnp.concatenate([jnp.where(hi, zero, y), jnp.where(hi, y, zero)], axis=0)


def _apply(x2, y, mm):
    return mm(x2, _bd(y))


def _xsel(a, e):
    h, m, l = _split3(a)
    return _mm(h, e) + (_mm(m, e) + _mm(l, e))


def _gate_select(base):
    r = _iota((LANES, 2 * LANES), 0)
    c = _iota((LANES, 2 * LANES), 1)
    src = base + ((c >> HEAD_SHIFT) & 1) + jnp.where(c >= LANES, 2 * DN_HEADS, 0)
    return jnp.where(r == src, 1.0, 0.0).astype(BF16)


_NEED_STATE = object()


def _neumann(n2, y):
    eye = (_iota(n2.shape, 1) & (HEAD_DIM - 1)) == _iota(n2.shape, 0)
    t = n2 + jnp.where(eye, 1.0, 0.0)
    pw = _apply(n2, n2, _mm_neu_sq)
    yield
    levels = CHUNK.bit_length() - 2
    for _ in range(levels):
        both = _apply(pw, jnp.concatenate([t, pw], axis=1), _mm_neu_sq)
        t = t + both[:, :LANES]
        pw = both[:, LANES:]
        yield
    return _apply(t, y, _mm_neu_y)


def _interleave(gens, first_state, pred):
    n = len(gens)
    results = [None] * n
    waiting = [False] * n
    done = [False] * n
    while not all(done):
        for i in range(n):
            if done[i]:
                continue
            try:
                if waiting[i]:
                    if pred[i] is None:
                        state = first_state[i]
                    elif done[pred[i]]:
                        state = results[pred[i]][1]
                    else:
                        continue
                    waiting[i] = False
                    token = gens[i].send(state)
                else:
                    token = next(gens[i])
                if token is _NEED_STATE:
                    waiting[i] = True
            except StopIteration as stop:
                results[i] = stop.value
                done[i] = True
    return results


def _dn_chunk(q2, k2, v2, gates, base, gc_row, cs):
    C = CHUNK
    gx = _xsel(gates, _gate_select(base))
    yield
    gcb, bb = gx[:, :LANES], gx[:, LANES:]
    kb = k2 * bb
    vb = v2 * bb
    s2 = _mm_score(jnp.concatenate([kb, q2], axis=0), _bd(k2), NT)
    row = cs["tot_row"]
    gtot = jnp.broadcast_to(gcb[row:row + 1, :], (C, LANES))
    eg = jnp.exp(gcb)
    kbe = kb * eg
    qg = q2 * eg
    kg = k2 * jnp.exp(gtot - gcb)
    rj = jnp.broadcast_to(gc_row, (C, LANES))
    yield
    dec = jnp.exp(jnp.where(cs["incl"], gcb - rj, NEG_BIG))
    l2 = jnp.where(cs["strict"], s2[:C] * dec, 0.0)
    qk = s2[C:] * dec
    y = yield from _neumann(-l2, jnp.concatenate([vb, kbe], axis=1))
    value, kcum = y[:, :LANES], y[:, LANES:]
    return value, kcum, qg, kg, qk, gtot


def _dn_state(load, cs):
    C = CHUNK
    P = yield _NEED_STATE
    sp = _mm_state(jnp.concatenate([load(1), load(2)], axis=0), P)
    yield
    vnew = load(0) - sp[:C]
    o = sp[C:] + _apply(load(4), vnew, _mm_state)
    upd = _mm_state(load(3), vnew, TN)
    yield
    gtot = load(5)
    glast = jnp.exp(jnp.concatenate([gtot, gtot], axis=0))
    pn = P * glast + jnp.where(cs["bdmask"], upd, 0.0)
    return o, pn


def _rw_chunk(r2, v2, at, gcb, kd2, b2, cs):
    C = CHUNK
    row = cs["tot_row"]
    gtot = jnp.broadcast_to(gcb[row:row + 1, :], (C, LANES))
    eneg = jnp.exp(-gcb)
    bt = b2 * eneg
    kt = kd2 * eneg
    rt = r2 * jnp.exp(gcb)
    er = jnp.exp(gtot - gcb)
    bh = b2 * er
    kh = kd2 * er
    bk = jnp.concatenate([_bd(bt), _bd(kt)], axis=0)
    sa = _mm_score(at, bk, NT)
    sr = _mm_score(rt, bk, NT)
    yield
    aab = jnp.where(cs["strict"], sa[:, :LANES], 0.0)
    aak = jnp.where(cs["strict"], sa[:, LANES:], 0.0)
    arb = jnp.where(cs["incl"], sr[:, :LANES], 0.0)
    ark = jnp.where(cs["incl"], sr[:, LANES:], 0.0)
    av = _apply(aak, v2, _mm_intra)
    arkv = _apply(ark, v2, _mm_intra)
    yield
    y = yield from _neumann(aab, jnp.concatenate([av, at], axis=1))
    vp, ap = y[:, :LANES], y[:, LANES:]
    return vp, ap, rt, arb, arkv, bh, kh, v2, gtot


def _rw_state(load, cs):
    C = CHUNK
    P = yield _NEED_STATE
    sp = _mm_state(jnp.concatenate([load(1), load(2)], axis=0), P)
    yield
    u = sp[:C] + load(0)
    v2 = load(7)
    out = sp[C:] + _apply(load(3), u, _mm_state) + load(4)
    upd = _mm_state(jnp.concatenate([load(5), load(6)], axis=0), jnp.concatenate([u, v2], axis=0), TN)
    yield
    gtot = load(8)
    wtot = jnp.exp(jnp.concatenate([gtot, gtot], axis=0))
    pn = P * wtot.T + jnp.where(cs["bdmask"], upd, 0.0)
    return out, pn


def _dn_prep_kernel(prev_ref, main_ref, next_ref, dab_ref, cw_ref, alog_ref, dt_ref,
                    qkv_ref, gates_ref, xbuf, *, tm):
    i = pl.program_id(1)
    n = pl.num_programs(1)
    halo = 8
    pad = (DN_CONV - 1) // 2
    xbuf[0:halo, :] = jnp.where(i == 0, 0.0, prev_ref[0])
    xbuf[halo:halo + tm, :] = main_ref[0]
    xbuf[halo + tm:, :] = jnp.where(i == n - 1, 0.0, next_ref[0])
    ones = _head_ones(LANES)
    for slab in range(3 * DN_WIDTH // LANES):
        cols = slice(slab * LANES, (slab + 1) * LANES)
        y = jnp.zeros((tm, LANES), F32)
        for j in range(DN_CONV):
            y = y + cw_ref[j:j + 1, cols] * xbuf[halo - pad + j:halo - pad + j + tm, cols]
        y = y * _sigmoid(y)
        if slab < 6:
            ss = _mmx(y * y, ones)
            y = y * lax.rsqrt(ss + 1e-6)
        if slab < 3:
            y = y * (HEAD_DIM ** -0.5)
        qkv_ref[0, :, cols] = y
    ab = dab_ref[0]
    lane = _iota((CHUNK, LANES), 1)
    g = -jnp.exp(alog_ref[...]) * _softplus(ab + dt_ref[...])
    beta = _sigmoid(ab)
    tm_f, tm_b = _cumsum_mats()
    for c in range(tm // CHUNK):
        rows = slice(c * CHUNK, (c + 1) * CHUNK)
        gc = jnp.where(lane < DN_HEADS, _xmm(tm_f, g[rows]), _xmm(tm_b, g[rows]))
        gates_ref[0, rows, :] = jnp.where(lane < 2 * DN_HEADS, gc, beta[rows])


def _dn_prep(dqkv3, dab3, conv_w, a_log, dt_bias, tm=1024):
    B, S, W = dqkv3.shape
    tm = min(tm, S)
    nb8 = tm // 8
    last8 = S // 8 - 1
    cw = jnp.pad(conv_w, ((0, 8 - DN_CONV), (0, 0)))
    pad_row = lambda a: jnp.pad(a.reshape(1, -1), ((0, 0), (0, LANES - a.size)))
    small = lambda a: pl.BlockSpec(a.shape, lambda b, i: (0,) * a.ndim)
    alog, dt = pad_row(a_log), pad_row(dt_bias)
    return pl.pallas_call(
        functools.partial(_dn_prep_kernel, tm=tm),
        grid=(B, S // tm),
        in_specs=[pl.BlockSpec((1, 8, W), lambda b, i: (b, jnp.maximum(i * nb8 - 1, 0), 0)),
                  pl.BlockSpec((1, tm, W), lambda b, i: (b, i, 0)),
                  pl.BlockSpec((1, 8, W), lambda b, i: (b, jnp.minimum((i + 1) * nb8, last8), 0)),
                  pl.BlockSpec((1, tm, LANES), lambda b, i: (b, i, 0)),
                  small(cw), small(alog), small(dt)],
        out_specs=[pl.BlockSpec((1, tm, W), lambda b, i: (b, i, 0)),
                   pl.BlockSpec((1, tm, LANES), lambda b, i: (b, i, 0))],
        out_shape=[jax.ShapeDtypeStruct((B, S, W), F32),
                   jax.ShapeDtypeStruct((B, S, LANES), F32)],
        scratch_shapes=[pltpu.VMEM((tm + 16, W), F32)],
        compiler_params=_params(("parallel", "parallel")),
        name="deltanet_prep",
    )(dqkv3, dqkv3, dqkv3, dab3, cw, alog, dt)


def _scan_step(state, insts, second):
    def whole(gen, cs, write):
        mid = yield from gen
        o, pn = yield from second(lambda a: mid[a], cs)
        write(o)
        return None, pn

    gens, pred, first = [], [], []
    last = {}
    for i, (chain, gen, cs, write) in enumerate(insts):
        gens.append(whole(gen, cs, write))
        pred.append(last.get(chain))
        first.append(None if chain in last else state[chain])
        last[chain] = i
    results = _interleave(gens, first, pred)
    for chain, i in last.items():
        state[chain] = results[i][1]


def _scan_reset(state):
    @pl.when(pl.program_id(1) == 0)
    def _():
        state[...] = jnp.zeros_like(state)


def _dn_scan_kernel(qf_ref, gf_ref, tf_ref, qb_ref, gb_ref, tb_ref, of_ref, ob_ref, state, *, cpb):
    _scan_reset(state)
    pairs = DN_HEADS // 2
    consts = [_chunk_consts(reverse=False), _chunk_consts(reverse=True)]
    insts = []
    for step in range(cpb):
        for d, (q_ref, g_ref, t_ref, o_ref) in enumerate(((qf_ref, gf_ref, tf_ref, of_ref),
                                                          (qb_ref, gb_ref, tb_ref, ob_ref))):
            c = step if d == 0 else cpb - 1 - step
            rows = slice(c * CHUNK, (c + 1) * CHUNK)
            gates = g_ref[0, rows, :]
            gc_rows = t_ref[0, c]
            for p in range(pairs):
                cols = slice(p * LANES, (p + 1) * LANES)
                q2 = q_ref[0, rows, p * LANES:(p + 1) * LANES]
                k2 = q_ref[0, rows, DN_WIDTH + p * LANES:DN_WIDTH + (p + 1) * LANES]
                v2 = q_ref[0, rows, 2 * DN_WIDTH + p * LANES:2 * DN_WIDTH + (p + 1) * LANES]
                gc_row = gc_rows[d * pairs + p:d * pairs + p + 1, :]
                gen = _dn_chunk(q2, k2, v2, gates, d * DN_HEADS + 2 * p, gc_row, consts[d])

                def write(o, o_ref=o_ref, rows=rows, cols=cols):
                    o_ref[0, rows, cols] = o

                insts.append(((d, p), gen, consts[d], write))
    _scan_step(state, insts, _dn_state)


def _scan_maps(nblk, ndim):
    tail = (0,) * (ndim - 2)
    return (lambda b, n: (b, n) + tail, lambda b, n: (b, nblk - 1 - n) + tail)


def _dn_scan(qkv3, gates3, cpb=8):
    B, S, W = qkv3.shape
    blk = cpb * CHUNK
    nblk = S // blk
    nchunk = S // CHUNK
    pairs = DN_HEADS // 2
    gct = gates3[..., :2 * DN_HEADS].reshape(B, nchunk, CHUNK, 2, pairs, 2)
    gct = gct.transpose(0, 1, 3, 4, 5, 2).reshape(B, nchunk, 2 * pairs, LANES)
    gct = jnp.pad(gct, ((0, 0), (0, 0), (0, 8 - 2 * pairs), (0, 0)))
    fwd, bwd = _scan_maps(nblk, 3)
    fwd4, bwd4 = _scan_maps(nblk, 4)
    return pl.pallas_call(
        functools.partial(_dn_scan_kernel, cpb=cpb),
        grid=(B, nblk),
        in_specs=[pl.BlockSpec((1, blk, W), fwd), pl.BlockSpec((1, blk, LANES), fwd),
                  pl.BlockSpec((1, cpb, 8, LANES), fwd4),
                  pl.BlockSpec((1, blk, W), bwd), pl.BlockSpec((1, blk, LANES), bwd),
                  pl.BlockSpec((1, cpb, 8, LANES), bwd4)],
        out_specs=[pl.BlockSpec((1, blk, DN_WIDTH), fwd), pl.BlockSpec((1, blk, DN_WIDTH), bwd)],
        out_shape=[jax.ShapeDtypeStruct((B, S, DN_WIDTH), F32)] * 2,
        scratch_shapes=[pltpu.VMEM((2, DN_HEADS // 2, LANES, LANES), F32)],
        compiler_params=_params(("parallel", "arbitrary")),
        name="deltanet_scan",
    )(qkv3, gates3, gct, qkv3, gates3, gct)


RW_OUTS = 12


def _rw_prep_kernel(prev_ref, main_ref, next_ref, mu_ref, w0_ref, a0_ref, kk_ref, ka_ref, rk_ref,
                    wup_ref, aup_ref, gup_ref, *refs, tm):
    outs = refs[:RW_OUTS]
    xbuf, pbuf = refs[RW_OUTS:]
    i = pl.program_id(1)
    n = pl.num_programs(1)
    halo = 8
    xbuf[0:halo, :] = jnp.where(i == 0, 0.0, prev_ref[0])
    xbuf[halo:halo + tm, :] = main_ref[0]
    xbuf[halo + tm:, :] = jnp.where(i == n - 1, 0.0, next_ref[0])
    for slab in range(RW_IN // LANES):
        cols = slice(slab * LANES, (slab + 1) * LANES)
        cur = xbuf[halo:halo + tm, cols]
        prv = xbuf[halo - 1:halo - 1 + tm, cols]
        nxt = xbuf[halo + 1:halo + 1 + tm, cols]
        pbuf[:, cols] = cur + mu_ref[0:1, cols] * (prv - cur) + mu_ref[1:2, cols] * (nxt - cur)
    r = pbuf[:, 0:RW_WIDTH]
    k = pbuf[:, RW_WIDTH:2 * RW_WIDTH]
    v = pbuf[:, 2 * RW_WIDTH:3 * RW_WIDTH]
    low = pbuf[:, 3 * RW_WIDTH:3 * RW_WIDTH + LANES]
    gd = pbuf[:, 3 * RW_WIDTH + LANES:]
    ones = _head_ones(RW_WIDTH)
    r_o, v_o, g_o, bg_o = outs[:4]
    g = _mm1(_sigmoid(gd), gup_ref[...])
    kq = k * kk_ref[...]
    kk = kq * lax.rsqrt(_mmx(kq * kq, ones) + 1e-6)
    bonus = _mmx(r * k * rk_ref[...], ones) * v
    r_o[0] = r
    v_o[0] = v
    g_o[0] = g
    bg_o[0] = bonus * g
    wl = jnp.tanh(low)
    cum = _cumsum_mats()
    for d in range(2):
        at_o, gc_o, kd_o, b_o = outs[4 + 4 * d:8 + 4 * d]
        w = -_softplus(-(w0_ref[d:d + 1, :] + _mm1(wl, wup_ref[d]))) - 0.5
        lw = -jnp.exp(w)
        for c in range(tm // CHUNK):
            rows = slice(c * CHUNK, (c + 1) * CHUNK)
            gc = _xmm(cum[d], lw[rows])
            gc_o[0, rows, :] = gc
            at_o[0, rows, :] = -kk[rows] * jnp.exp(gc - lw[rows])
        a = _sigmoid(a0_ref[d:d + 1, :] + _mm1(low, aup_ref[d]))
        kd_o[0] = k * (1.0 + (a - 1.0) * ka_ref[...])
        b_o[0] = kk * a


def _rw_prep(rwp3, mu, w0, w_up, a0, a_up, g_up, k_k, k_a, r_k, tm=512):
    B, S, W = rwp3.shape
    tm = min(tm, S)
    nb8 = tm // 8
    last8 = S // 8 - 1
    zeros = jnp.zeros((2, W_RANK, RW_WIDTH), F32)
    wup = jnp.concatenate([w_up, zeros], axis=1)
    aup = jnp.concatenate([zeros, a_up], axis=1)
    rowv = lambda a: a.reshape(1, RW_WIDTH)
    small = lambda a: pl.BlockSpec(a.shape, lambda b, i: (0,) * a.ndim)
    params = [mu, w0, a0, rowv(k_k), rowv(k_a), rowv(r_k), wup, aup, g_up]
    tile = pl.BlockSpec((1, tm, RW_WIDTH), lambda b, i: (b, i, 0))
    return pl.pallas_call(
        functools.partial(_rw_prep_kernel, tm=tm),
        grid=(B, S // tm),
        in_specs=[pl.BlockSpec((1, 8, W), lambda b, i: (b, jnp.maximum(i * nb8 - 1, 0), 0)),
                  pl.BlockSpec((1, tm, W), lambda b, i: (b, i, 0)),
                  pl.BlockSpec((1, 8, W), lambda b, i: (b, jnp.minimum((i + 1) * nb8, last8), 0))]
                 + [small(a) for a in params],
        out_specs=[tile] * RW_OUTS,
        out_shape=[jax.ShapeDtypeStruct((B, S, RW_WIDTH), F32)] * RW_OUTS,
        scratch_shapes=[pltpu.VMEM((tm + 16, W), F32), pltpu.VMEM((tm, W), F32)],
        compiler_params=_params(("parallel", "parallel")),
        name="rwkv_prep",
    )(rwp3, rwp3, rwp3, *params)


def _rw_scan_kernel(*refs, cpb):
    ins = refs[:12]
    yf_ref, yb_ref, state = refs[12:]
    _scan_reset(state)
    pairs = RW_HEADS // 2
    consts = [_chunk_consts(reverse=False), _chunk_consts(reverse=True)]
    insts = []
    for step in range(cpb):
        for d, o_ref in enumerate((yf_ref, yb_ref)):
            src = ins[6 * d:6 * d + 6]
            c = step if d == 0 else cpb - 1 - step
            rows = slice(c * CHUNK, (c + 1) * CHUNK)
            for p in range(pairs):
                cols = slice(p * LANES, (p + 1) * LANES)
                gen = _rw_chunk(*[ref[0, rows, cols] for ref in src], consts[d])

                def write(o, o_ref=o_ref, rows=rows, cols=cols):
                    o_ref[0, rows, cols] = o

                insts.append(((d, p), gen, consts[d], write))
    _scan_step(state, insts, _rw_state)


def _rw_scan(r, v, dirs, cpb=8):
    B, S, W = r.shape
    blk = cpb * CHUNK
    nblk = S // blk
    fwd, bwd = [pl.BlockSpec((1, blk, W), m) for m in _scan_maps(nblk, 3)]
    return pl.pallas_call(
        functools.partial(_rw_scan_kernel, cpb=cpb),
        grid=(B, nblk),
        in_specs=[fwd] * 6 + [bwd] * 6,
        out_specs=[fwd, bwd],
        out_shape=[jax.ShapeDtypeStruct((B, S, W), F32)] * 2,
        scratch_shapes=[pltpu.VMEM((2, RW_HEADS // 2, LANES, LANES), F32)],
        compiler_params=_params(("parallel", "arbitrary")),
        name="rwkv_scan",
    )(r, v, *dirs[0], r, v, *dirs[1])


def _out_proj_kernel(x_ref, ya_ref, of_ref, ob_ref, z_ref, dnw_ref, yf_ref, yb_ref, g_ref, bg_ref,
                     lnw_ref, lnb_ref, wa_ref, wd_ref, wr_ref, gain_ref, xo_ref, h_ref):
    o = of_ref[...] + ob_ref[...]
    ms = _mmx(o * o, _head_ones(DN_WIDTH)) * (1.0 / HEAD_DIM)
    z = z_ref[...]
    ydn = o * lax.rsqrt(ms + NORM_EPS) * dnw_ref[...] * (z * _sigmoid(z))
    y = yf_ref[...] + yb_ref[...]
    ones = _head_ones(RW_WIDTH)
    mean = _mmx(y, ones) * (1.0 / HEAD_DIM)
    yc = y - mean
    var = _mmx(yc * yc, ones) * (1.0 / HEAD_DIM)
    yrw = (yc * lax.rsqrt(var + RW_LN_EPS) * lnw_ref[...] + lnb_ref[...]) * g_ref[...] + bg_ref[...]
    acc = _mm(ya_ref[...], wa_ref[...])
    acc = acc + _mm(ydn.astype(BF16), wd_ref[...])
    acc = acc + _mm(yrw.astype(BF16), wr_ref[...])
    xn = x_ref[...] + acc
    xo_ref[...] = xn
    ms2 = jnp.mean(xn * xn, axis=-1, keepdims=True)
    h_ref[...] = (xn * lax.rsqrt(ms2 + NORM_EPS) * gain_ref[...]).astype(BF16)


def _out_proj(x2, ya, of, ob, z, dnw, yf, yb, g, bg, lnw, lnb, w_out, gain, tm=512):
    T = x2.shape[0]
    tm = min(tm, T)
    row = lambda a: pl.BlockSpec((tm, a.shape[1]), lambda i: (i, 0))
    full = lambda a: pl.BlockSpec(a.shape, lambda i: (0,) * a.ndim)
    wa = w_out[:ATT_WIDTH].astype(BF16)
    wd = w_out[ATT_WIDTH:ATT_WIDTH + DN_WIDTH].astype(BF16)
    wr = w_out[ATT_WIDTH + DN_WIDTH:].astype(BF16)
    args = [x2, ya, of, ob, z, dnw, yf, yb, g, bg, lnw, lnb, wa, wd, wr, gain]
    is_row = [True, True, True, True, True, False, True, True, True, True, False, False,
              False, False, False, False]
    return pl.pallas_call(
        _out_proj_kernel,
        grid=(T // tm,),
        in_specs=[row(a) if r else full(a) for a, r in zip(args, is_row)],
        out_specs=[pl.BlockSpec((tm, D_MODEL), lambda i: (i, 0))] * 2,
        out_shape=[jax.ShapeDtypeStruct((T, D_MODEL), F32), jax.ShapeDtypeStruct((T, D_MODEL), BF16)],
        compiler_params=_params(("parallel",)),
        name="out_proj",
    )(*args)


FFN_HALO = 16


def _ffn_kernel(prev_ref, main_ref, next_ref, x_ref, wg_ref, wv_ref, cg_ref, cv_ref, wd_ref, fin_ref,
                o_ref, *, tm, final_norm):
    i = pl.program_id(1)
    ni = pl.num_programs(1)
    zero = jnp.zeros((FFN_HALO, D_MODEL), BF16)
    h = jnp.concatenate([jnp.where(i == 0, zero, prev_ref[0]), main_ref[0],
                         jnp.where(i == ni - 1, zero, next_ref[0])], axis=0)
    rows = tm + 2 * FFN_HALO

    def conv(u, c_ref):
        um = pltpu.roll(u, 1, 0)
        up = pltpu.roll(u, rows - 1, 0)
        y = c_ref[0:1, :] * um + c_ref[1:2, :] * u + c_ref[2:3, :] * up
        return y[FFN_HALO:FFN_HALO + tm]

    gate = conv(_mm(h, wg_ref[...]), cg_ref)
    val = conv(_mm(h, wv_ref[...]), cv_ref)
    act = (gate * _sigmoid(gate) * val).astype(BF16)
    xn = x_ref[0] + _mm(act, wd_ref[...])
    if final_norm:
        ms = jnp.mean(xn * xn, axis=-1, keepdims=True)
        xn = xn * lax.rsqrt(ms + NORM_EPS) * fin_ref[...]
    o_ref[0] = xn


def _ffn(h3, x3, w_up, conv_w, w_down, fin_gain, final_norm, tm=512):
    B, S, _ = h3.shape
    tm = min(tm, S)
    nbh = tm // FFN_HALO
    lasth = S // FFN_HALO - 1
    wup = w_up.astype(BF16)
    wdn = w_down.astype(BF16)
    cw = jnp.pad(conv_w, ((0, 8 - conv_w.shape[0]), (0, 0)))
    once = pl.Buffered(1)
    return pl.pallas_call(
        functools.partial(_ffn_kernel, tm=tm, final_norm=final_norm),
        grid=(B, S // tm),
        in_specs=[
            pl.BlockSpec((1, FFN_HALO, D_MODEL), lambda b, i: (b, jnp.maximum(i * nbh - 1, 0), 0)),
            pl.BlockSpec((1, tm, D_MODEL), lambda b, i: (b, i, 0)),
            pl.BlockSpec((1, FFN_HALO, D_MODEL), lambda b, i: (b, jnp.minimum((i + 1) * nbh, lasth), 0)),
            pl.BlockSpec((1, tm, D_MODEL), lambda b, i: (b, i, 0)),
            pl.BlockSpec((D_MODEL, D_FF), lambda b, i: (0, 0), pipeline_mode=once),
            pl.BlockSpec((D_MODEL, D_FF), lambda b, i: (0, 1), pipeline_mode=once),
            pl.BlockSpec((8, D_FF), lambda b, i: (0, 0), pipeline_mode=once),
            pl.BlockSpec((8, D_FF), lambda b, i: (0, 1), pipeline_mode=once),
            pl.BlockSpec((D_FF, D_MODEL), lambda b, i: (0, 0), pipeline_mode=once),
            pl.BlockSpec((1, D_MODEL), lambda b, i: (0, 0)),
        ],
        out_specs=pl.BlockSpec((1, tm, D_MODEL), lambda b, i: (b, i, 0)),
        out_shape=jax.ShapeDtypeStruct((B, S, D_MODEL), F32),
        compiler_params=_params(("parallel", "parallel")),
        name="conv_ffn",
    )(h3, h3, h3, x3, wup, wup, cw, cw, wdn, fin_gain)


def _in_weights(w_in_l):
    a = ATT_WIDTH + 2 * ATT_KV_WIDTH
    d = a + 3 * DN_WIDTH
    z = d + DN_WIDTH
    g = z + 4 * DN_HEADS
    w_ab = jnp.pad(w_in_l[:, z:g], ((0, 0), (0, LANES - 4 * DN_HEADS)))
    groups = (w_in_l[:, :a], w_in_l[:, a:d], w_in_l[:, d:z], w_ab, w_in_l[:, g:])
    return [w.astype(BF16) for w in groups]


def kernel(x, positions, norm_mix, w_in, attn_sink, dn_conv, dn_a_log, dn_dt_bias, dn_norm, rw_mu, rw_w0, rw_w_up, rw_a0, rw_a_up, rw_g_up, rw_k_k, rw_k_a, rw_r_k, rw_ln_w, rw_ln_b, w_out, norm_ffn, ffn_w_up, ffn_conv, ffn_w_down, norm_final):
    B, S, D = x.shape
    T = B * S
    depth = w_in.shape[0]
    tables = _rope_tables(positions)
    x2 = x.reshape(T, D)
    for l in range(depth):
        att, dqkv, dz, dab, rwp = _in_proj(x2, norm_mix[l].reshape(1, D), _in_weights(w_in[l]), tables)
        y_att = _attention(att.reshape(B, S, ATT_OUT), attn_sink[l])
        qkv_act, gates = _dn_prep(dqkv.reshape(B, S, -1), dab.reshape(B, S, LANES),
                                  dn_conv[l], dn_a_log[l], dn_dt_bias[l])
        o_f, o_b = _dn_scan(qkv_act, gates)
        rw = _rw_prep(rwp.reshape(B, S, RW_IN), rw_mu[l], rw_w0[l], rw_w_up[l], rw_a0[l], rw_a_up[l],
                      rw_g_up[l], rw_k_k[l], rw_k_a[l], rw_r_k[l])
        r, v, g, bg = rw[:4]
        y_f, y_b = _rw_scan(r, v, (rw[4:8], rw[8:12]))
        flat = lambda a: a.reshape(T, a.shape[-1])
        x2, h2 = _out_proj(
            x2, flat(y_att), flat(o_f), flat(o_b), dz,
            jnp.tile(dn_norm[l], DN_HEADS).reshape(1, DN_WIDTH),
            flat(y_f), flat(y_b), flat(g), flat(bg),
            rw_ln_w[l].reshape(1, RW_WIDTH), rw_ln_b[l].reshape(1, RW_WIDTH),
            w_out[l], norm_ffn[l].reshape(1, D))
        x2 = _ffn(h2.reshape(B, S, D), x2.reshape(B, S, D), ffn_w_up[l], ffn_conv[l], ffn_w_down[l],
                  norm_final.reshape(1, D), final_norm=(l == depth - 1)).reshape(T, D)
    return x2.reshape(B, S, D)
```
